```python
import jax, jax.numpy as jnp
from jax import lax
import numpy as np

D_MODEL = 1024
BATCH = 8
SEQ = 8192
DEPTH = 4

CHUNK = 64
D_MIX = D_MODEL
D_CONV = D_MIX // 4
D_POOL = D_MIX // 4
D_ATTN = D_MIX // 2
HEAD_DIM = 64
N_HEADS = D_ATTN // HEAD_DIM
CONV_W = 3
POOL_WINDOWS = (2, 4, 8, 16)
N_POOL = len(POOL_WINDOWS)
POOL_GC = D_POOL // N_POOL
LEFT_CHUNKS = 8
BAND = (LEFT_CHUNKS + 1) * CHUNK
REL_CLIP = 128
D_FF = ((8 * D_MODEL // 3 + 255) // 256) * 256
IN_COLS = 3 * D_CONV + D_POOL + 3 * D_ATTN
EPS = 1e-6

kernel_name = "hybrid_conv_pool_chunkattn_trunk"


def rmsnorm(x, g):
    xf = x.astype(jnp.float32)
    y = xf * lax.rsqrt(jnp.mean(xf * xf, axis=-1, keepdims=True) + EPS)
    return (y * g.astype(jnp.float32)).astype(x.dtype)


def shift_right(u, n):
    s = u.shape[1]
    return jnp.pad(u, ((0, 0), (n, 0), (0, 0)))[:, :s]


def short_conv_mixer(gb, gc, u, conv_w):
    z = gc * u
    conv = conv_w[2] * z + conv_w[1] * shift_right(z, 1) + conv_w[0] * shift_right(z, 2)
    return gb * conv


def pool_mixer(u, pool_w, pool_scale):
    b, s, _ = u.shape
    uf = u.astype(jnp.float32)
    cs = jnp.cumsum(uf, axis=1)
    t = jnp.arange(s)
    outs = []
    for gi, w in enumerate(POOL_WINDOWS):
        sl = slice(gi * POOL_GC, (gi + 1) * POOL_GC)
        c = cs[..., sl]
        cnt = jnp.minimum(t + 1, w).astype(jnp.float32)[None, :, None]
        outs.append((c - shift_right(c, w)) / cnt - uf[..., sl])
    d = jnp.concatenate(outs, axis=-1).astype(u.dtype).reshape(b, s, N_POOL, POOL_GC)
    y = jnp.einsum('bsgc,gcd->bsgd', d, pool_w).reshape(b, s, D_POOL)
    return y * pool_scale


def chunk_attention(q, k, v, rel_bias):
    b, s, h, dh = q.shape
    n_chunks = s // CHUNK
    pad = ((0, 0), (LEFT_CHUNKS * CHUNK, 0), (0, 0), (0, 0))
    kpad = jnp.pad(k, pad)
    vpad = jnp.pad(v, pad)
    qi = jnp.arange(CHUNK)[:, None]
    kj = jnp.arange(BAND)[None, :]
    rel = LEFT_CHUNKS * CHUNK + qi - kj
    idx = jnp.clip(rel, -REL_CLIP, REL_CLIP) + REL_CLIP
    bias = rel_bias[:, idx].astype(jnp.float32)
    key_off = jnp.arange(BAND)
    scale = HEAD_DIM ** -0.5

    def one_chunk(c):
        qc = lax.dynamic_slice_in_dim(q, c * CHUNK, CHUNK, axis=1)
        kb = lax.dynamic_slice_in_dim(kpad, c * CHUNK, BAND, axis=1)
        vb = lax.dynamic_slice_in_dim(vpad, c * CHUNK, BAND, axis=1)
        sc = jnp.einsum('bqhd,bkhd->bhqk', qc, kb).astype(jnp.float32) * scale + bias[None]
        valid = key_off >= (LEFT_CHUNKS - c) * CHUNK
        sc = jnp.where(valid, sc, jnp.finfo(jnp.float32).min)
        p = jax.nn.softmax(sc, axis=-1).astype(vb.dtype)
        return jnp.einsum('bhqk,bkhd->bqhd', p, vb)

    out = lax.map(one_chunk, jnp.arange(n_chunks))
    return out.transpose(1, 0, 2, 3, 4).reshape(b, s, h * dh)


def setup_inputs(seed: int = 0) -> dict:
    key = jax.random.key(seed)
    ks = jax.random.split(key, 16)
    f32 = jnp.float32
    nrm = lambda k, shp, sc: jax.random.normal(k, shp, f32) * sc
    gain = lambda k, shp: 1.0 + 0.05 * jax.random.normal(k, shp, f32)
    return {
        "x": jax.random.normal(ks[0], (BATCH, SEQ, D_MODEL), f32),
        "w_in": nrm(ks[1], (DEPTH, D_MODEL, IN_COLS), D_MODEL ** -0.5),
        "w_out": nrm(ks[2], (DEPTH, D_MIX, D_MODEL), D_MIX ** -0.5),
        "conv_w": nrm(ks[3], (DEPTH, CONV_W, D_CONV), CONV_W ** -0.5),
        "pool_w": nrm(ks[4], (DEPTH, N_POOL, POOL_GC, POOL_GC), POOL_GC ** -0.5),
        "pool_scale": gain(ks[5], (DEPTH, D_POOL)),
        "rel_bias": nrm(ks[6], (DEPTH, N_HEADS, 2 * REL_CLIP + 1), 0.1),
        "group_gain": gain(ks[7], (DEPTH, D_MIX)),
        "pre_mix_g": gain(ks[8], (DEPTH, D_MODEL)),
        "post_mix_g": gain(ks[9], (DEPTH, D_MODEL)),
        "pre_ffn_g": gain(ks[10], (DEPTH, D_MODEL)),
        "post_ffn_g": gain(ks[11], (DEPTH, D_MODEL)),
        "w_gate_up": nrm(ks[12], (DEPTH, D_MODEL, 2 * D_FF), D_MODEL ** -0.5),
        "w_down": nrm(ks[13], (DEPTH, D_FF, D_MODEL), D_FF ** -0.5),
    }


def reference(x, w_in, w_out, conv_w, pool_w, pool_scale, rel_bias, group_gain,
              pre_mix_g, post_mix_g, pre_ffn_g, post_ffn_g, w_gate_up, w_down):
    b, s, _ = x.shape
    h = x
    for l in range(DEPTH):
        xn = rmsnorm(h, pre_mix_g[l])
        proj = jnp.einsum('bsd,dc->bsc', xn, w_in[l])
        o = 0
        gb = proj[..., o:o + D_CONV]; o += D_CONV
        gc = proj[..., o:o + D_CONV]; o += D_CONV
        u = proj[..., o:o + D_CONV]; o += D_CONV
        pu = proj[..., o:o + D_POOL]; o += D_POOL
        q = proj[..., o:o + D_ATTN].reshape(b, s, N_HEADS, HEAD_DIM); o += D_ATTN
        k = proj[..., o:o + D_ATTN].reshape(b, s, N_HEADS, HEAD_DIM); o += D_ATTN
        v = proj[..., o:o + D_ATTN].reshape(b, s, N_HEADS, HEAD_DIM)

        ya = short_conv_mixer(gb, gc, u, conv_w[l])
        yb = pool_mixer(pu, pool_w[l], pool_scale[l])
        yc = chunk_attention(q, k, v, rel_bias[l])

        gg = group_gain[l]
        ya = rmsnorm(ya, gg[:D_CONV])
        yb = rmsnorm(yb, gg[D_CONV:D_CONV + D_POOL])
        yc = rmsnorm(yc, gg[D_CONV + D_POOL:])
        y = jnp.concatenate([ya, yb, yc], axis=-1)
        mix = jnp.einsum('bsc,cd->bsd', y, w_out[l])
        h = h + rmsnorm(mix, post_mix_g[l])

        hn = rmsnorm(h, pre_ffn_g[l])
        gu = jnp.einsum('bsd,df->bsf', hn, w_gate_up[l])
        ff = jax.nn.silu(gu[..., :D_FF]) * gu[..., D_FF:]
        ffo = jnp.einsum('bsf,fd->bsd', ff, w_down[l])
        h = h + rmsnorm(ffo, post_ffn_g[l])
    return h
```

```python
import functools

import numpy as np
import jax
import jax.numpy as jnp
from jax import lax
from jax.experimental import pallas as pl
from jax.experimental.pallas import tpu as pltpu

CHUNK = 64
HEAD_DIM = 64
CONV_W = 3
POOL_WINDOWS = (2, 4, 8, 16)
LEFT_CHUNKS = 8
REL_CLIP = 128
EPS = 1e-6

LEFT = LEFT_CHUNKS * CHUNK
QBLK = 4 * CHUNK
KBLK = QBLK + LEFT
POOL_HALO = 16
CONV_HALO = 8
NEG = -1e30

MIX_TILE = 512
FFN_TILE = 512
FFN_CHUNK = 1024
VMEM_LIMIT_BYTES = 56 * 1024 * 1024

_F32 = jnp.float32
_BF16 = jnp.bfloat16


def _rms(x, g):
    return x * lax.rsqrt(jnp.mean(x * x, axis=-1, keepdims=True) + EPS) * g


def _dot(a, b):
    return jnp.dot(a, b, preferred_element_type=_F32)


def _mix_kernel(x_ref, win_ref, wout_ref, convw_ref, poolw_ref, pscale_ref, bias_ref,
                gg_ref, g1_ref, g2_ref, o_ref, kbuf, vbuf, zbuf, pbuf, *, tile, n_heads):
    t = pl.program_id(1)
    d_conv = zbuf.shape[1]
    d_pool = pbuf.shape[1]
    d_attn = kbuf.shape[1]
    c0 = 3 * d_conv + d_pool

    @pl.when(t == 0)
    def _():
        kbuf[0:LEFT, :] = jnp.zeros((LEFT, d_attn), _BF16)
        vbuf[0:LEFT, :] = jnp.zeros((LEFT, d_attn), _BF16)
        zbuf[0:CONV_HALO, :] = jnp.zeros((CONV_HALO, d_conv), _F32)
        pbuf[0:POOL_HALO, :] = jnp.zeros((POOL_HALO, d_pool), _F32)

    x = x_ref[...]
    xn = _rms(x, g1_ref[...]).astype(_BF16)

    pa = _dot(xn, win_ref[:, 0:c0])
    gb = pa[:, 0:d_conv]
    z = pa[:, d_conv:2 * d_conv] * pa[:, 2 * d_conv:3 * d_conv]
    pu = pa[:, 3 * d_conv:c0]
    zbuf[CONV_HALO:CONV_HALO + tile, :] = z
    conv = convw_ref[2:3, :] * z
    conv = conv + convw_ref[1:2, :] * zbuf[CONV_HALO - 1:CONV_HALO - 1 + tile, :]
    conv = conv + convw_ref[0:1, :] * zbuf[CONV_HALO - 2:CONV_HALO - 2 + tile, :]
    ya = gb * conv
    zbuf[0:CONV_HALO, :] = zbuf[tile:tile + CONV_HALO, :]

    pbuf[POOL_HALO:POOL_HALO + tile, :] = pu
    gc_pool = d_pool // len(POOL_WINDOWS)
    lane_group = lax.broadcasted_iota(jnp.int32, (tile, d_pool), 1) // gc_pool
    pos = t * tile + lax.broadcasted_iota(jnp.int32, (tile, d_pool), 0)
    wsum = None
    win = None
    run = pu
    done = 1
    for gi, w in enumerate(POOL_WINDOWS):
        for j in range(done, w):
            run = run + pbuf[POOL_HALO - j:POOL_HALO - j + tile, :]
        done = w
        if gi == 0:
            wsum = run
            win = jnp.full((tile, d_pool), w, jnp.int32)
        else:
            wsum = jnp.where(lane_group >= gi, run, wsum)
            win = jnp.where(lane_group >= gi, w, win)
    cnt = jnp.minimum(pos + 1, win).astype(_F32)
    dpool = wsum / cnt - pu
    yb = _dot(dpool.astype(_BF16), poolw_ref[...]) * pscale_ref[...]
    pbuf[0:POOL_HALO, :] = pbuf[tile:tile + POOL_HALO, :]

    qkv = _dot(xn, win_ref[:, c0:c0 + 3 * d_attn])
    q = (qkv[:, 0:d_attn] * (HEAD_DIM ** -0.5)).astype(_BF16)
    kbuf[LEFT:LEFT + tile, :] = qkv[:, d_attn:2 * d_attn].astype(_BF16)
    vbuf[LEFT:LEFT + tile, :] = qkv[:, 2 * d_attn:3 * d_attn].astype(_BF16)
    col = lax.broadcasted_iota(jnp.int32, (QBLK, KBLK), 1)
    blocks = []
    for qb in range(tile // QBLK):
        r0 = qb * QBLK
        first_valid = LEFT - t * tile - r0
        heads = []
        for h in range(n_heads):
            hs = slice(h * HEAD_DIM, (h + 1) * HEAD_DIM)
            s = lax.dot_general(q[r0:r0 + QBLK, hs], kbuf[r0:r0 + KBLK, hs],
                                (((1,), (1,)), ((), ())), preferred_element_type=_F32)
            s = jnp.where(col >= first_valid, s + bias_ref[h], NEG)
            m = jnp.max(s, axis=-1, keepdims=True)
            p = jnp.exp(s - m)
            l = jnp.sum(p, axis=-1, keepdims=True)
            o = _dot(p.astype(_BF16), vbuf[r0:r0 + KBLK, hs])
            heads.append(o / l)
        blocks.append(jnp.concatenate(heads, axis=-1))
    yc = jnp.concatenate(blocks, axis=0) if len(blocks) > 1 else blocks[0]
    kbuf[0:LEFT, :] = kbuf[tile:tile + LEFT, :]
    vbuf[0:LEFT, :] = vbuf[tile:tile + LEFT, :]

    gg = gg_ref[...]
    y = jnp.concatenate([
        _rms(ya, gg[:, 0:d_conv]),
        _rms(yb, gg[:, d_conv:d_conv + d_pool]),
        _rms(yc, gg[:, d_conv + d_pool:]),
    ], axis=-1).astype(_BF16)
    mix = _dot(y, wout_ref[...])
    o_ref[...] = x + _rms(mix, g2_ref[...])


def _ffn_kernel(h_ref, g1_ref, g2_ref, wgu_ref, wd_ref, o_ref, *, d_ff, chunk):
    x = h_ref[...]
    hn = _rms(x, g1_ref[...]).astype(_BF16)
    acc = None
    for f0 in range(0, d_ff, chunk):
        f1 = min(f0 + chunk, d_ff)
        g = _dot(hn, wgu_ref[:, f0:f1])
        u = _dot(hn, wgu_ref[:, d_ff + f0:d_ff + f1])
        ff = (g * (1.0 / (1.0 + jnp.exp(-g))) * u).astype(_BF16)
        part = _dot(ff, wd_ref[f0:f1, :])
        acc = part if acc is None else acc + part
    o_ref[...] = x + _rms(acc, g2_ref[...])


def _bias_table(rel_bias):
    qpos = np.arange(QBLK)[:, None]
    kk = np.arange(KBLK)[None, :]
    kj = kk - (qpos // CHUNK) * CHUNK
    rel = LEFT + qpos % CHUNK - kj
    idx = np.clip(rel, -REL_CLIP, REL_CLIP) + REL_CLIP
    in_band = (kj >= 0) & (kj < LEFT + CHUNK)
    return jnp.where(in_band, rel_bias[:, :, idx], NEG).astype(_F32)


def _pool_block_diag(pool_w):
    n_layers, n_groups, c, _ = pool_w.shape
    eye = jnp.eye(n_groups, dtype=pool_w.dtype)
    bd = pool_w[:, :, :, None, :] * eye[None, :, None, :, None]
    return bd.reshape(n_layers, n_groups * c, n_groups * c)


def _resident(shape, layer):
    zeros = (0,) * (len(shape) - 1)
    return pl.BlockSpec((None,) + tuple(shape[1:]), lambda *_: (layer,) + zeros,
                        pipeline_mode=pl.Buffered(1))


def kernel(x, w_in, w_out, conv_w, pool_w, pool_scale, rel_bias, group_gain, pre_mix_g,
           post_mix_g, pre_ffn_g, post_ffn_g, w_gate_up, w_down):
    b, s, d_model = x.shape
    depth = w_in.shape[0]
    d_conv = conv_w.shape[2]
    d_pool = pool_scale.shape[1]
    n_heads = rel_bias.shape[1]
    d_attn = n_heads * HEAD_DIM
    d_ff = w_down.shape[1]
    assert s % MIX_TILE == 0 and MIX_TILE % QBLK == 0 and MIX_TILE >= LEFT
    assert (b * s) % FFN_TILE == 0
    assert w_in.shape[2] == 3 * d_conv + d_pool + 3 * d_attn

    w_in_b = w_in.astype(_BF16)
    w_out_b = w_out.astype(_BF16)
    w_gu_b = w_gate_up.astype(_BF16)
    w_down_b = w_down.astype(_BF16)
    pool_bd = _pool_block_diag(pool_w).astype(_BF16)
    bias_tbl = _bias_table(rel_bias)
    row = lambda a: a.reshape(depth, 1, a.shape[-1])
    pool_scale, group_gain, pre_mix_g, post_mix_g, pre_ffn_g, post_ffn_g = map(
        row, (pool_scale, group_gain, pre_mix_g, post_mix_g, pre_ffn_g, post_ffn_g))

    params = pltpu.CompilerParams(dimension_semantics=("arbitrary", "arbitrary"),
                                  vmem_limit_bytes=VMEM_LIMIT_BYTES)
    ffn_params = pltpu.CompilerParams(dimension_semantics=("arbitrary",),
                                      vmem_limit_bytes=VMEM_LIMIT_BYTES)
    x_spec = pl.BlockSpec((None, MIX_TILE, d_model), lambda bi, ti: (bi, ti, 0))
    tok_spec = pl.BlockSpec((FFN_TILE, d_model), lambda i: (i, 0))

    h = x
    for l in range(depth):
        mix_in = (w_in_b, w_out_b, conv_w, pool_bd, pool_scale, bias_tbl, group_gain, pre_mix_g, post_mix_g)
        h = pl.pallas_call(
            functools.partial(_mix_kernel, tile=MIX_TILE, n_heads=n_heads),
            out_shape=jax.ShapeDtypeStruct((b, s, d_model), _F32),
            grid=(b, s // MIX_TILE),
            in_specs=[x_spec] + [_resident(a.shape, l) for a in mix_in],
            out_specs=x_spec,
            scratch_shapes=[
                pltpu.VMEM((LEFT + MIX_TILE, d_attn), _BF16),
                pltpu.VMEM((LEFT + MIX_TILE, d_attn), _BF16),
                pltpu.VMEM((CONV_HALO + MIX_TILE, d_conv), _F32),
                pltpu.VMEM((POOL_HALO + MIX_TILE, d_pool), _F32),
            ],
            compiler_params=params,
            name=f"mix{l}",
        )(h, *mix_in)

        ffn_in = (pre_ffn_g, post_ffn_g, w_gu_b, w_down_b)
        h = pl.pallas_call(
            functools.partial(_ffn_kernel, d_ff=d_ff, chunk=FFN_CHUNK),
            out_shape=jax.ShapeDtypeStruct((b * s, d_model), _F32),
            grid=((b * s) // FFN_TILE,),
            in_specs=[tok_spec] + [_resident(a.shape, l) for a in ffn_in],
            out_specs=tok_spec,
            compiler_params=ffn_params,
            name=f"ffn{l}",
        )(h.reshape(b * s, d_model), *ffn_in).reshape(b, s, d_model)
    return h
```

```python
import functools

import numpy as np
import jax
import jax.numpy as jnp
from jax import lax
from jax.experimental import pallas as pl
from jax.experimental.pallas import tpu as pltpu

CHUNK = 64
HEAD_DIM = 64
CONV_W = 3
POOL_WINDOWS = (2, 4, 8, 16)
LEFT_CHUNKS = 8
REL_CLIP = 128
EPS = 1e-6

LANES = 128
LEFT = LEFT_CHUNKS * CHUNK
QBLK = 4 * CHUNK
KBLK = QBLK + LEFT
SLAB = 32
POOL_HALO = 16
CONV_HALO = 8
NEG = -1e30

MIX_TILE = 512
FFN_TILE = 512
FFN_CHUNK = 1024
VMEM_LIMIT_BYTES = 56 * 1024 * 1024

_F32 = jnp.float32
_BF16 = jnp.bfloat16


def _rms(x, g):
    return x * lax.rsqrt(jnp.mean(x * x, axis=-1, keepdims=True) + EPS) * g


def _dot(a, b):
    return jnp.dot(a, b, preferred_element_type=_F32)


def _dot_nt(a, b):
    return lax.dot_general(a, b, (((1,), (1,)), ((), ())), preferred_element_type=_F32)


def _mix_kernel(x_ref, win_ref, wout_ref, convw_ref, poolw_ref, pscale_ref, bias_ref,
                gg_ref, g1_ref, g2_ref, o_ref, kbuf, vbuf, zbuf, pbuf, s_ref, p_ref,
                *, tile, n_heads):
    t = pl.program_id(1)
    d_conv = zbuf.shape[1]
    d_pool = pbuf.shape[1]
    d_attn = kbuf.shape[2]
    c0 = 3 * d_conv + d_pool

    @pl.when(t == 0)
    def _():
        kbuf[:, 0:LEFT, :] = jnp.zeros((2, LEFT, d_attn), _BF16)
        vbuf[:, 0:LEFT, :] = jnp.zeros((2, LEFT, d_attn), _BF16)
        zbuf[0:CONV_HALO, :] = jnp.zeros((CONV_HALO, d_conv), _F32)
        pbuf[0:POOL_HALO, :] = jnp.zeros((POOL_HALO, d_pool), _F32)

    x = x_ref[...]
    xn = _rms(x, g1_ref[...]).astype(_BF16)

    pa = _dot(xn, win_ref[:, 0:c0])
    gb = pa[:, 0:d_conv]
    z = pa[:, d_conv:2 * d_conv] * pa[:, 2 * d_conv:3 * d_conv]
    pu = pa[:, 3 * d_conv:c0]
    zbuf[CONV_HALO:CONV_HALO + tile, :] = z
    ze = zbuf[...]
    conv = convw_ref[2:3, :] * z
    conv = conv + convw_ref[1:2, :] * pltpu.roll(ze, 1, axis=0)[CONV_HALO:, :]
    conv = conv + convw_ref[0:1, :] * pltpu.roll(ze, 2, axis=0)[CONV_HALO:, :]
    ya = gb * conv
    zbuf[0:CONV_HALO, :] = ze[tile:tile + CONV_HALO, :]

    pbuf[POOL_HALO:POOL_HALO + tile, :] = pu
    pe = pbuf[...]
    assert POOL_WINDOWS == (2, 4, 8, 16) and d_pool == 2 * LANES
    w2 = pe + pltpu.roll(pe, 1, axis=0)
    w4 = w2 + pltpu.roll(w2, 2, axis=0)
    w4hi = w4[:, LANES:]
    w8 = w4hi + pltpu.roll(w4hi, 4, axis=0)
    w16 = w8 + pltpu.roll(w8, 8, axis=0)
    low_half = lax.broadcasted_iota(jnp.int32, (tile, LANES), 1) < LANES // 2
    wsum = jnp.concatenate([
        jnp.where(low_half, w2[POOL_HALO:, :LANES], w4[POOL_HALO:, :LANES]),
        jnp.where(low_half, w8[POOL_HALO:, :], w16[POOL_HALO:, :]),
    ], axis=-1)
    win = jnp.concatenate([jnp.where(low_half, 2, 4), jnp.where(low_half, 8, 16)], axis=-1)
    pos = t * tile + lax.broadcasted_iota(jnp.int32, (tile, d_pool), 0)
    cnt = jnp.minimum(pos + 1, win).astype(_F32)
    dpool = wsum / cnt - pu
    yb = _dot(dpool.astype(_BF16), poolw_ref[...]) * pscale_ref[...]
    pbuf[0:POOL_HALO, :] = pe[tile:tile + POOL_HALO, :]

    qkv = _dot(xn, win_ref[:, c0:c0 + 3 * d_attn])
    q = (qkv[:, 0:d_attn] * (HEAD_DIM ** -0.5)).astype(_BF16)
    k = qkv[:, d_attn:2 * d_attn]
    v = qkv[:, 2 * d_attn:3 * d_attn]
    even_head = (lax.broadcasted_iota(jnp.int32, (tile, d_attn), 1) // HEAD_DIM) % 2 == 0
    kbuf[0, LEFT:LEFT + tile, :] = jnp.where(even_head, k, 0.0).astype(_BF16)
    kbuf[1, LEFT:LEFT + tile, :] = jnp.where(even_head, 0.0, k).astype(_BF16)
    vbuf[0, LEFT:LEFT + tile, :] = jnp.where(even_head, v, 0.0).astype(_BF16)
    vbuf[1, LEFT:LEFT + tile, :] = jnp.where(even_head, 0.0, v).astype(_BF16)
    krow = lax.broadcasted_iota(jnp.int32, (LEFT + tile, LANES), 0)
    klane = lax.broadcasted_iota(jnp.int32, (LEFT + tile, LANES), 1)
    kneg = jnp.where((klane == 0) & (t * tile + krow < LEFT), NEG, 0.0).astype(_BF16)
    qone = jnp.where(lax.broadcasted_iota(jnp.int32, (QBLK, LANES), 1) == 0, 1.0, 0.0).astype(_BF16)

    blocks = []
    for qb in range(tile // QBLK):
        r0 = qb * QBLK
        pairs = []
        for pr in range(n_heads // 2):
            ls = slice(pr * LANES, (pr + 1) * LANES)
            q_aug = jnp.concatenate([q[r0:r0 + QBLK, ls], qone], axis=-1)
            y_pair = None
            for e in range(2):
                h = 2 * pr + e
                buf = h % 2
                k_aug = jnp.concatenate([kbuf[e, r0:r0 + KBLK, ls], kneg[r0:r0 + KBLK, :]], axis=-1)
                s_ref[buf] = _dot_nt(q_aug, k_aug) + bias_ref[h]
                linv = []
                for i in range(QBLK // SLAB):
                    rows = slice(i * SLAB, (i + 1) * SLAB)
                    s = s_ref[buf, rows, :]
                    p = jnp.exp(s - jnp.max(s, axis=-1, keepdims=True))
                    linv.append(1.0 / jnp.sum(p, axis=-1, keepdims=True))
                    p_ref[buf, rows, :] = p.astype(_BF16)
                o = _dot(p_ref[buf], vbuf[e, r0:r0 + KBLK, ls]) * jnp.concatenate(linv, axis=0)
                y_pair = o if y_pair is None else y_pair + o
            pairs.append(y_pair)
        blocks.append(jnp.concatenate(pairs, axis=-1))
    yc = jnp.concatenate(blocks, axis=0) if len(blocks) > 1 else blocks[0]
    kbuf[:, 0:LEFT, :] = kbuf[:, tile:tile + LEFT, :]
    vbuf[:, 0:LEFT, :] = vbuf[:, tile:tile + LEFT, :]

    gg = gg_ref[...]
    y = jnp.concatenate([
        _rms(ya, gg[:, 0:d_conv]),
        _rms(yb, gg[:, d_conv:d_conv + d_pool]),
        _rms(yc, gg[:, d_conv + d_pool:]),
    ], axis=-1).astype(_BF16)
    mix = _dot(y, wout_ref[...])
    o_ref[...] = x + _rms(mix, g2_ref[...])


def _ffn_kernel(h_ref, g1_ref, g2_ref, wgu_ref, wd_ref, o_ref, *, d_ff, chunk):
    x = h_ref[...]
    hn = _rms(x, g1_ref[...]).astype(_BF16)
    acc = None
    for f0 in range(0, d_ff, chunk):
        f1 = min(f0 + chunk, d_ff)
        g = _dot(hn, wgu_ref[:, f0:f1])
        u = _dot(hn, wgu_ref[:, d_ff + f0:d_ff + f1])
        ff = (g * (1.0 / (1.0 + jnp.exp(-g))) * u).astype(_BF16)
        part = _dot(ff, wd_ref[f0:f1, :])
        acc = part if acc is None else acc + part
    o_ref[...] = x + _rms(acc, g2_ref[...])


def _bias_table(rel_bias):
    period = 1 << (QBLK + KBLK - 1).bit_length()
    assert period >= QBLK + KBLK
    lead = rel_bias.shape[:-1]
    far = jnp.broadcast_to(rel_bias[..., 2 * REL_CLIP:], lead + (LEFT - REL_CLIP + 1,))
    ramp = rel_bias[..., 2 * REL_CLIP - 1::-1]
    near = jnp.broadcast_to(rel_bias[..., :1], lead + (KBLK - LEFT - REL_CLIP,))
    wrap = jnp.broadcast_to(rel_bias[..., 2 * REL_CLIP:], lead + (period - KBLK - 1,))
    vec = jnp.concatenate([far, ramp, near, wrap], axis=-1)
    assert vec.shape[-1] == period
    flat = jnp.tile(vec, (1,) * len(lead) + (QBLK,))[..., :QBLK * (period - 1)]
    toeplitz = flat.reshape(lead + (QBLK, period - 1))[..., :KBLK]
    kj = np.arange(KBLK)[None, :] - (np.arange(QBLK)[:, None] // CHUNK) * CHUNK
    in_band = (kj >= 0) & (kj < LEFT + CHUNK)
    return jnp.where(in_band, toeplitz, NEG).astype(_F32)


def _pool_block_diag(pool_w):
    n_layers, n_groups, c, _ = pool_w.shape
    eye = jnp.eye(n_groups, dtype=pool_w.dtype)
    bd = pool_w[:, :, :, None, :] * eye[None, :, None, :, None]
    return bd.reshape(n_layers, n_groups * c, n_groups * c)


def _resident(shape, layer):
    zeros = (0,) * (len(shape) - 1)
    return pl.BlockSpec((None,) + tuple(shape[1:]), lambda *_: (layer,) + zeros,
                        pipeline_mode=pl.Buffered(1))


def kernel(x, w_in, w_out, conv_w, pool_w, pool_scale, rel_bias, group_gain, pre_mix_g,
           post_mix_g, pre_ffn_g, post_ffn_g, w_gate_up, w_down):
    b, s, d_model = x.shape
    depth = w_in.shape[0]
    d_conv = conv_w.shape[2]
    d_pool = pool_scale.shape[1]
    n_heads = rel_bias.shape[1]
    d_attn = n_heads * HEAD_DIM
    d_ff = w_down.shape[1]
    assert s % MIX_TILE == 0 and MIX_TILE % QBLK == 0 and MIX_TILE >= LEFT
    assert (b * s) % FFN_TILE == 0 and n_heads % 2 == 0 and 2 * HEAD_DIM == LANES
    assert w_in.shape[2] == 3 * d_conv + d_pool + 3 * d_attn

    w_in_b = w_in.astype(_BF16)
    w_out_b = w_out.astype(_BF16)
    w_gu_b = w_gate_up.astype(_BF16)
    w_down_b = w_down.astype(_BF16)
    pool_bd = _pool_block_diag(pool_w).astype(_BF16)
    bias_tbl = _bias_table(rel_bias)
    row = lambda a: a.reshape(depth, 1, a.shape[-1])
    pool_scale, group_gain, pre_mix_g, post_mix_g, pre_ffn_g, post_ffn_g = map(
        row, (pool_scale, group_gain, pre_mix_g, post_mix_g, pre_ffn_g, post_ffn_g))

    params = pltpu.CompilerParams(dimension_semantics=("arbitrary", "arbitrary"),
                                  vmem_limit_bytes=VMEM_LIMIT_BYTES)
    ffn_params = pltpu.CompilerParams(dimension_semantics=("arbitrary",),
                                      vmem_limit_bytes=VMEM_LIMIT_BYTES)
    x_spec = pl.BlockSpec((None, MIX_TILE, d_model), lambda bi, ti: (bi, ti, 0))
    tok_spec = pl.BlockSpec((FFN_TILE, d_model), lambda i: (i, 0))

    h = x
    for l in range(depth):
        mix_in = (w_in_b, w_out_b, conv_w, pool_bd, pool_scale, bias_tbl, group_gain, pre_mix_g, post_mix_g)
        h = pl.pallas_call(
            functools.partial(_mix_kernel, tile=MIX_TILE, n_heads=n_heads),
            out_shape=jax.ShapeDtypeStruct((b, s, d_model), _F32),
            grid=(b, s // MIX_TILE),
            in_specs=[x_spec] + [_resident(a.shape, l) for a in mix_in],
            out_specs=x_spec,
            scratch_shapes=[
                pltpu.VMEM((2, LEFT + MIX_TILE, d_attn), _BF16),
                pltpu.VMEM((2, LEFT + MIX_TILE, d_attn), _BF16),
                pltpu.VMEM((CONV_HALO + MIX_TILE, d_conv), _F32),
                pltpu.VMEM((POOL_HALO + MIX_TILE, d_pool), _F32),
                pltpu.VMEM((2, QBLK, KBLK), _F32),
                pltpu.VMEM((2, QBLK, KBLK), _BF16),
            ],
            compiler_params=params,
            name=f"mix{l}",
        )(h, *mix_in)

        ffn_in = (pre_ffn_g, post_ffn_g, w_gu_b, w_down_b)
        h = pl.pallas_call(
            functools.partial(_ffn_kernel, d_ff=d_ff, chunk=FFN_CHUNK),
            out_shape=jax.ShapeDtypeStruct((b * s, d_model), _F32),
            grid=((b * s) // FFN_TILE,),
            in_specs=[tok_spec] + [_resident(a.shape, l) for a in ffn_in],
            out_specs=tok_spec,
            compiler_params=ffn_params,
            name=f"ffn{l}",
        )(h.reshape(b * s, d_model), *ffn_in).reshape(b, s, d_model)
    return h
```

```python
import functools

import numpy as np
import jax
import jax.numpy as jnp
from jax import lax
from jax.experimental import pallas as pl
from jax.experimental.pallas import tpu as pltpu

CHUNK = 64
HEAD_DIM = 64
CONV_W = 3
POOL_WINDOWS = (2, 4, 8, 16)
LEFT_CHUNKS = 8
REL_CLIP = 128
EPS = 1e-6

LANES = 128
LEFT = LEFT_CHUNKS * CHUNK
QBLK = 4 * CHUNK
KBLK = QBLK + LEFT
SLAB = 32
N_SLOTS = 4
POOL_HALO = 16
CONV_HALO = 8
NEG = -1e30

MIX_TILE = 512
FFN_TILE = 512
FFN_CHUNK = 1024
VMEM_LIMIT_BYTES = 56 * 1024 * 1024

_F32 = jnp.float32
_BF16 = jnp.bfloat16


def _rms(x, g):
    return x * lax.rsqrt(jnp.mean(x * x, axis=-1, keepdims=True) + EPS) * g


def _dot(a, b):
    return jnp.dot(a, b, preferred_element_type=_F32)


def _dot_nt(a, b):
    return lax.dot_general(a, b, (((1,), (1,)), ((), ())), preferred_element_type=_F32)


def _mix_kernel(x_ref, win_ref, wout_ref, convw_ref, poolw_ref, pscale_ref, bias_ref,
                gg_ref, g1_ref, g2_ref, o_ref, q_s, kbuf, vbuf, zbuf, pbuf, s_buf, p_buf,
                linv_buf, yc_s, *, tile, n_heads):
    t = pl.program_id(1)
    d_conv = zbuf.shape[1]
    d_pool = pbuf.shape[1]
    n_pairs = n_heads // 2
    d_attn = n_heads * HEAD_DIM
    c0 = 3 * d_conv + d_pool

    @pl.when(t == 0)
    def _():
        lane0 = lax.broadcasted_iota(jnp.int32, (LEFT + tile, LANES), 1) == 0
        missing = lax.broadcasted_iota(jnp.int32, (LEFT + tile, LANES), 0) < LEFT
        kneg = jnp.where(lane0 & missing, NEG, 0.0).astype(_BF16)
        qone = jnp.where(lax.broadcasted_iota(jnp.int32, (tile, LANES), 1) == 0, 1.0, 0.0).astype(_BF16)
        for pr in range(n_pairs):
            q_s[pr, :, LANES:] = qone
            for e in range(2):
                kbuf[e, pr, 0:LEFT, 0:LANES] = jnp.zeros((LEFT, LANES), _BF16)
                kbuf[e, pr, :, LANES:] = kneg
            vbuf[pr, 0:LEFT, :] = jnp.zeros((LEFT, LANES), _BF16)
        zbuf[0:CONV_HALO, :] = jnp.zeros((CONV_HALO, d_conv), _F32)
        pbuf[0:POOL_HALO, :] = jnp.zeros((POOL_HALO, d_pool), _F32)
        p_buf[...] = jnp.zeros(p_buf.shape, _BF16)

    @pl.when(t == 1)
    def _():
        for pr in range(n_pairs):
            for e in range(2):
                kbuf[e, pr, 0:LEFT, LANES:] = jnp.zeros((LEFT, LANES), _BF16)

    x = x_ref[...]
    xn = _rms(x, g1_ref[...]).astype(_BF16)

    pa = _dot(xn, win_ref[:, 0:c0])
    gb = pa[:, 0:d_conv]
    z = pa[:, d_conv:2 * d_conv] * pa[:, 2 * d_conv:3 * d_conv]
    pu = pa[:, 3 * d_conv:c0]
    zbuf[CONV_HALO:CONV_HALO + tile, :] = z
    ze = zbuf[...]
    conv = convw_ref[2:3, :] * z
    conv = conv + convw_ref[1:2, :] * pltpu.roll(ze, 1, axis=0)[CONV_HALO:, :]
    conv = conv + convw_ref[0:1, :] * pltpu.roll(ze, 2, axis=0)[CONV_HALO:, :]
    ya = gb * conv
    zbuf[0:CONV_HALO, :] = ze[tile:tile + CONV_HALO, :]

    pbuf[POOL_HALO:POOL_HALO + tile, :] = pu
    pe = pbuf[...]
    assert POOL_WINDOWS == (2, 4, 8, 16) and d_pool == 2 * LANES
    w2 = pe + pltpu.roll(pe, 1, axis=0)
    w4 = w2 + pltpu.roll(w2, 2, axis=0)
    w4hi = w4[:, LANES:]
    w8 = w4hi + pltpu.roll(w4hi, 4, axis=0)
    w16 = w8 + pltpu.roll(w8, 8, axis=0)
    low_half = lax.broadcasted_iota(jnp.int32, (tile, LANES), 1) < LANES // 2
    wsum = jnp.concatenate([
        jnp.where(low_half, w2[POOL_HALO:, :LANES], w4[POOL_HALO:, :LANES]),
        jnp.where(low_half, w8[POOL_HALO:, :], w16[POOL_HALO:, :]),
    ], axis=-1)
    win = jnp.concatenate([jnp.where(low_half, 2, 4), jnp.where(low_half, 8, 16)], axis=-1)
    pos = t * tile + lax.broadcasted_iota(jnp.int32, (tile, d_pool), 0)
    cnt = jnp.minimum(pos + 1, win).astype(_F32)
    dpool = wsum / cnt - pu
    yb = _dot(dpool.astype(_BF16), poolw_ref[...]) * pscale_ref[...]
    pbuf[0:POOL_HALO, :] = pe[tile:tile + POOL_HALO, :]

    qkv = _dot(xn, win_ref[:, c0:c0 + 3 * d_attn])
    q = (qkv[:, 0:d_attn] * (HEAD_DIM ** -0.5)).astype(_BF16)
    k = qkv[:, d_attn:2 * d_attn]
    v = qkv[:, 2 * d_attn:3 * d_attn].astype(_BF16)
    even_head = (lax.broadcasted_iota(jnp.int32, (tile, d_attn), 1) // HEAD_DIM) % 2 == 0
    k_even = jnp.where(even_head, k, 0.0).astype(_BF16)
    k_odd = jnp.where(even_head, 0.0, k).astype(_BF16)
    for pr in range(n_pairs):
        ls = slice(pr * LANES, (pr + 1) * LANES)
        q_s[pr, :, 0:LANES] = q[:, ls]
        kbuf[0, pr, LEFT:LEFT + tile, 0:LANES] = k_even[:, ls]
        kbuf[1, pr, LEFT:LEFT + tile, 0:LANES] = k_odd[:, ls]
        vbuf[pr, LEFT:LEFT + tile, :] = v[:, ls]

    low_lanes = lax.broadcasted_iota(jnp.int32, (QBLK, LANES), 1) < HEAD_DIM
    n_steps = (tile // QBLK) * n_heads

    def step_index(i):
        pair = (i // 2) % n_pairs
        r0 = (i // n_heads) * QBLK
        if not isinstance(i, int):
            r0 = pl.multiple_of(r0, QBLK)
        return pair, r0

    def band_lanes(row):
        first = (row // CHUNK) * CHUNK
        return pl.ds(first // LANES * LANES, -(-(LEFT + CHUNK + first % LANES) // LANES) * LANES)

    def scores(i, slot):
        pair, r0 = step_index(i)
        s = _dot_nt(q_s[pair, pl.ds(r0, QBLK), :], kbuf[slot % 2, pair, pl.ds(r0, KBLK), :])
        for c0_row in range(0, QBLK, CHUNK):
            rows, cols = pl.ds(c0_row, CHUNK), band_lanes(c0_row)
            s_buf[slot, rows, cols] = s[c0_row:c0_row + CHUNK, cols.start:cols.start + cols.size] \
                + bias_ref[i % n_heads, rows, cols]

    def softmax(slot):
        for j in range(QBLK // SLAB):
            rows, cols = pl.ds(j * SLAB, SLAB), band_lanes(j * SLAB)
            s = s_buf[slot, rows, cols]
            p = jnp.exp(s - jnp.max(s, axis=-1, keepdims=True))
            linv = 1.0 / jnp.sum(p, axis=-1, keepdims=True)
            p_buf[slot, rows, cols] = p.astype(_BF16)
            linv_buf[slot, rows, :] = jnp.broadcast_to(linv, (SLAB, LANES))

    def weighted_values(i, slot):
        pair, r0 = step_index(i)
        o = _dot(p_buf[slot], vbuf[pair, pl.ds(r0, KBLK), :]) * linv_buf[slot]
        rows = pl.ds(r0, QBLK)
        if slot % 2 == 0:
            yc_s[pair, rows, :] = o
        else:
            yc_s[pair, rows, :] = jnp.where(low_lanes, yc_s[pair, rows, :], o)

    def pipeline_body(i0, slot0, do_pv, do_sm, do_qk):
        if do_sm:
            softmax(slot0)
            softmax(slot0 + 1)
        if do_pv:
            weighted_values(i0 - 2, (slot0 + 2) % N_SLOTS)
            weighted_values(i0 - 1, (slot0 + 3) % N_SLOTS)
        if do_qk:
            scores(i0 + 2, (slot0 + 2) % N_SLOTS)
            scores(i0 + 3, (slot0 + 3) % N_SLOTS)

    assert n_steps % N_SLOTS == 0 and n_steps >= 2 * N_SLOTS and N_SLOTS == 4
    pipeline_body(-2, 2, False, False, True)
    pipeline_body(0, 0, False, True, True)

    def steady(j, carry):
        pipeline_body(N_SLOTS * j + 2, 2, True, True, True)
        pipeline_body(N_SLOTS * j + 4, 0, True, True, True)
        return carry

    lax.fori_loop(0, n_steps // N_SLOTS - 1, steady, 0)
    pipeline_body(n_steps - 2, 2, True, True, False)
    pipeline_body(n_steps, 0, True, False, False)

    yc = jnp.concatenate([yc_s[pr] for pr in range(n_pairs)], axis=-1)
    kbuf[:, :, 0:LEFT, 0:LANES] = kbuf[:, :, tile:tile + LEFT, 0:LANES]
    vbuf[:, 0:LEFT, :] = vbuf[:, tile:tile + LEFT, :]

    gg = gg_ref[...]
    y = jnp.concatenate([
        _rms(ya, gg[:, 0:d_conv]),
        _rms(yb, gg[:, d_conv:d_conv + d_pool]),
        _rms(yc, gg[:, d_conv + d_pool:]),
    ], axis=-1).astype(_BF16)
    mix = _dot(y, wout_ref[...])
    o_ref[...] = x + _rms(mix, g2_ref[...])


def _ffn_kernel(h_ref, g1_ref, g2_ref, wgu_ref, wd_ref, o_ref, *, d_ff, chunk):
    half = h_ref.shape[0] // 2
    for r in (0, half):
        x = h_ref[r:r + half, :]
        hn = _rms(x, g1_ref[...]).astype(_BF16)
        acc = None
        for f0 in range(0, d_ff, chunk):
            f1 = min(f0 + chunk, d_ff)
            g = _dot(hn, wgu_ref[:, f0:f1])
            u = _dot(hn, wgu_ref[:, d_ff + f0:d_ff + f1])
            ff = (g * (1.0 / (1.0 + jnp.exp(-g))) * u).astype(_BF16)
            part = _dot(ff, wd_ref[f0:f1, :])
            acc = part if acc is None else acc + part
        o_ref[r:r + half, :] = x + _rms(acc, g2_ref[...])


def _bias_table(rel_bias):
    period = 1 << (QBLK + KBLK - 1).bit_length()
    assert period >= QBLK + KBLK
    lead = rel_bias.shape[:-1]
    far = jnp.broadcast_to(rel_bias[..., 2 * REL_CLIP:], lead + (LEFT - REL_CLIP + 1,))
    ramp = rel_bias[..., 2 * REL_CLIP - 1::-1]
    near = jnp.broadcast_to(rel_bias[..., :1], lead + (KBLK - LEFT - REL_CLIP,))
    wrap = jnp.broadcast_to(rel_bias[..., 2 * REL_CLIP:], lead + (period - KBLK - 1,))
    vec = jnp.concatenate([far, ramp, near, wrap], axis=-1)
    assert vec.shape[-1] == period
    flat = jnp.tile(vec, (1,) * len(lead) + (QBLK,))[..., :QBLK * (period - 1)]
    toeplitz = flat.reshape(lead + (QBLK, period - 1))[..., :KBLK]
    kj = np.arange(KBLK)[None, :] - (np.arange(QBLK)[:, None] // CHUNK) * CHUNK
    in_band = (kj >= 0) & (kj < LEFT + CHUNK)
    return jnp.where(in_band, toeplitz, NEG).astype(_F32)


def _pool_block_diag(pool_w):
    n_layers, n_groups, c, _ = pool_w.shape
    eye = jnp.eye(n_groups, dtype=pool_w.dtype)
    bd = pool_w[:, :, :, None, :] * eye[None, :, None, :, None]
    return bd.reshape(n_layers, n_groups * c, n_groups * c)


def _resident(shape, layer):
    zeros = (0,) * (len(shape) - 1)
    return pl.BlockSpec((None,) + tuple(shape[1:]), lambda *_: (layer,) + zeros,
                        pipeline_mode=pl.Buffered(1))


def kernel(x, w_in, w_out, conv_w, pool_w, pool_scale, rel_bias, group_gain, pre_mix_g,
           post_mix_g, pre_ffn_g, post_ffn_g, w_gate_up, w_down):
    b, s, d_model = x.shape
    depth = w_in.shape[0]
    d_conv = conv_w.shape[2]
    d_pool = pool_scale.shape[1]
    n_heads = rel_bias.shape[1]
    d_attn = n_heads * HEAD_DIM
    n_pairs = n_heads // 2
    d_ff = w_down.shape[1]
    assert s % MIX_TILE == 0 and MIX_TILE % QBLK == 0 and MIX_TILE >= LEFT
    assert (b * s) % FFN_TILE == 0 and n_heads % 2 == 0 and 2 * HEAD_DIM == LANES
    assert w_in.shape[2] == 3 * d_conv + d_pool + 3 * d_attn

    w_in_b = w_in.astype(_BF16)
    w_out_b = w_out.astype(_BF16)
    w_gu_b = w_gate_up.astype(_BF16)
    w_down_b = w_down.astype(_BF16)
    pool_bd = _pool_block_diag(pool_w).astype(_BF16)
    bias_tbl = _bias_table(rel_bias)
    row = lambda a: a.reshape(depth, 1, a.shape[-1])
    pool_scale, group_gain, pre_mix_g, post_mix_g, pre_ffn_g, post_ffn_g = map(
        row, (pool_scale, group_gain, pre_mix_g, post_mix_g, pre_ffn_g, post_ffn_g))

    params = pltpu.CompilerParams(dimension_semantics=("arbitrary", "arbitrary"),
                                  vmem_limit_bytes=VMEM_LIMIT_BYTES)
    ffn_params = pltpu.CompilerParams(dimension_semantics=("arbitrary",),
                                      vmem_limit_bytes=VMEM_LIMIT_BYTES)
    x_spec = pl.BlockSpec((None, MIX_TILE, d_model), lambda bi, ti: (bi, ti, 0))
    tok_spec = pl.BlockSpec((FFN_TILE, d_model), lambda i: (i, 0))

    h = x
    for l in range(depth):
        mix_in = (w_in_b, w_out_b, conv_w, pool_bd, pool_scale, bias_tbl, group_gain, pre_mix_g, post_mix_g)
        h = pl.pallas_call(
            functools.partial(_mix_kernel, tile=MIX_TILE, n_heads=n_heads),
            out_shape=jax.ShapeDtypeStruct((b, s, d_model), _F32),
            grid=(b, s // MIX_TILE),
            in_specs=[x_spec] + [_resident(a.shape, l) for a in mix_in],
            out_specs=x_spec,
            scratch_shapes=[
                pltpu.VMEM((n_pairs, MIX_TILE, 2 * LANES), _BF16),
                pltpu.VMEM((2, n_pairs, LEFT + MIX_TILE, 2 * LANES), _BF16),
                pltpu.VMEM((n_pairs, LEFT + MIX_TILE, LANES), _BF16),
                pltpu.VMEM((CONV_HALO + MIX_TILE, d_conv), _F32),
                pltpu.VMEM((POOL_HALO + MIX_TILE, d_pool), _F32),
                pltpu.VMEM((N_SLOTS, QBLK, KBLK), _F32),
                pltpu.VMEM((N_SLOTS, QBLK, KBLK), _BF16),
                pltpu.VMEM((N_SLOTS, QBLK, LANES), _F32),
                pltpu.VMEM((n_pairs, MIX_TILE, LANES), _F32),
            ],
            compiler_params=params,
            name=f"mix{l}",
        )(h, *mix_in)

        ffn_in = (pre_ffn_g, post_ffn_g, w_gu_b, w_down_b)
        h = pl.pallas_call(
            functools.partial(_ffn_kernel, d_ff=d_ff, chunk=FFN_CHUNK),
            out_shape=jax.ShapeDtypeStruct((b * s, d_model), _F32),
            grid=((b * s) // FFN_TILE,),
            in_specs=[tok_spec] + [_resident(a.shape, l) for a in ffn_in],
            out_specs=tok_spec,
            compiler_params=ffn_params,
            name=f"ffn{l}",
        )(h.reshape(b * s, d_model), *ffn_in).reshape(b, s, d_model)
    return h
```

```python
import functools

import numpy as np
import jax
import jax.numpy as jnp
from jax import lax
from jax.experimental import pallas as pl
from jax.experimental.pallas import tpu as pltpu

CHUNK = 64
HEAD_DIM = 64
CONV_W = 3
POOL_WINDOWS = (2, 4, 8, 16)
LEFT_CHUNKS = 8
REL_CLIP = 128
EPS = 1e-6

LANES = 128
LEFT = LEFT_CHUNKS * CHUNK
QBLK = 4 * CHUNK
KBLK = QBLK + LEFT
SLAB = 32
N_SLOTS = 4
POOL_HALO = 16
CONV_HALO = 8
NEG = -1e30

MIX_TILE = 512
FFN_TILE = 1024
FFN_ROWS = 256
FFN_CHUNK = 1024
VMEM_LIMIT_BYTES = 56 * 1024 * 1024

_F32 = jnp.float32
_BF16 = jnp.bfloat16


def _rms(x, g):
    return x * lax.rsqrt(jnp.mean(x * x, axis=-1, keepdims=True) + EPS) * g


def _dot(a, b):
    return jnp.dot(a, b, preferred_element_type=_F32)


def _dot_nt(a, b):
    return lax.dot_general(a, b, (((1,), (1,)), ((), ())), preferred_element_type=_F32)


def _mix_kernel(x_ref, win_ref, wout_ref, convw_ref, poolw_ref, pscale_ref, bias_ref,
                gg_ref, g1_ref, g2_ref, o_ref, q_s, kbuf, vbuf, zbuf, pbuf, s_buf, p_buf,
                yc_s, *, tile, n_heads):
    t = pl.program_id(1)
    d_conv = zbuf.shape[1]
    d_pool = pbuf.shape[1]
    n_pairs = n_heads // 2
    d_attn = n_heads * HEAD_DIM
    c0 = 3 * d_conv + d_pool

    @pl.when(t > 0)
    def _():
        kbuf[:, :, 0:LEFT, 0:LANES] = kbuf[:, :, tile:tile + LEFT, 0:LANES]
        vbuf[:, 0:LEFT, 0:LANES] = vbuf[:, tile:tile + LEFT, 0:LANES]

    @pl.when(t == 0)
    def _():
        lane0 = lax.broadcasted_iota(jnp.int32, (LEFT + tile, LANES), 1) == 0
        missing = lax.broadcasted_iota(jnp.int32, (LEFT + tile, LANES), 0) < LEFT
        kneg = jnp.where(lane0 & missing, NEG, 0.0).astype(_BF16)
        one = jnp.where(lane0, 1.0, 0.0).astype(_BF16)
        qone = jnp.where(lax.broadcasted_iota(jnp.int32, (tile, LANES), 1) == 0, 1.0, 0.0).astype(_BF16)
        for pr in range(n_pairs):
            q_s[pr, :, LANES:] = qone
            for e in range(2):
                kbuf[e, pr, 0:LEFT, 0:LANES] = jnp.zeros((LEFT, LANES), _BF16)
                kbuf[e, pr, :, LANES:] = kneg
            vbuf[pr, 0:LEFT, 0:LANES] = jnp.zeros((LEFT, LANES), _BF16)
            vbuf[pr, :, LANES:] = one
        zbuf[0:CONV_HALO, :] = jnp.zeros((CONV_HALO, d_conv), _F32)
        pbuf[0:POOL_HALO, :] = jnp.zeros((POOL_HALO, d_pool), _F32)
        p_buf[...] = jnp.zeros(p_buf.shape, _BF16)

    @pl.when(t == 1)
    def _():
        for pr in range(n_pairs):
            for e in range(2):
                kbuf[e, pr, 0:LEFT, LANES:] = jnp.zeros((LEFT, LANES), _BF16)

    x = x_ref[...]
    xn = _rms(x, g1_ref[...]).astype(_BF16)

    pa = _dot(xn, win_ref[:, 0:c0])

    def conv_and_pool():
        gb = pa[:, 0:d_conv]
        z = pa[:, d_conv:2 * d_conv] * pa[:, 2 * d_conv:3 * d_conv]
        pu = pa[:, 3 * d_conv:c0]
        zbuf[CONV_HALO:CONV_HALO + tile, :] = z
        ze = zbuf[...]
        conv = convw_ref[2:3, :] * z
        conv = conv + convw_ref[1:2, :] * pltpu.roll(ze, 1, axis=0)[CONV_HALO:, :]
        conv = conv + convw_ref[0:1, :] * pltpu.roll(ze, 2, axis=0)[CONV_HALO:, :]
        ya = gb * conv
        zbuf[0:CONV_HALO, :] = ze[tile:tile + CONV_HALO, :]

        pbuf[POOL_HALO:POOL_HALO + tile, :] = pu
        pe = pbuf[...]
        assert POOL_WINDOWS == (2, 4, 8, 16) and d_pool == 2 * LANES
        w2 = pe + pltpu.roll(pe, 1, axis=0)
        w4 = w2 + pltpu.roll(w2, 2, axis=0)
        w4hi = w4[:, LANES:]
        w8 = w4hi + pltpu.roll(w4hi, 4, axis=0)
        w16 = w8 + pltpu.roll(w8, 8, axis=0)
        low_half = lax.broadcasted_iota(jnp.int32, (tile, LANES), 1) < LANES // 2
        wsum = jnp.concatenate([
            jnp.where(low_half, w2[POOL_HALO:, :LANES], w4[POOL_HALO:, :LANES]),
            jnp.where(low_half, w8[POOL_HALO:, :], w16[POOL_HALO:, :]),
        ], axis=-1)
        win = jnp.concatenate([jnp.where(low_half, 2, 4), jnp.where(low_half, 8, 16)], axis=-1)
        pos = t * tile + lax.broadcasted_iota(jnp.int32, (tile, d_pool), 0)
        cnt = jnp.minimum(pos + 1, win).astype(_F32)
        dpool = wsum / cnt - pu
        yb = _dot(dpool.astype(_BF16), poolw_ref[...]) * pscale_ref[...]
        pbuf[0:POOL_HALO, :] = pe[tile:tile + POOL_HALO, :]
        return ya, yb

    qkv = _dot(xn, win_ref[:, c0:c0 + 3 * d_attn])
    q = (qkv[:, 0:d_attn] * (HEAD_DIM ** -0.5)).astype(_BF16)
    k = qkv[:, d_attn:2 * d_attn]
    v = qkv[:, 2 * d_attn:3 * d_attn].astype(_BF16)
    even_head = (lax.broadcasted_iota(jnp.int32, (tile, d_attn), 1) // HEAD_DIM) % 2 == 0
    k_even = jnp.where(even_head, k, 0.0).astype(_BF16)
    k_odd = jnp.where(even_head, 0.0, k).astype(_BF16)
    for pr in range(n_pairs):
        ls = slice(pr * LANES, (pr + 1) * LANES)
        q_s[pr, :, 0:LANES] = q[:, ls]
        kbuf[0, pr, LEFT:LEFT + tile, 0:LANES] = k_even[:, ls]
        kbuf[1, pr, LEFT:LEFT + tile, 0:LANES] = k_odd[:, ls]
        vbuf[pr, LEFT:LEFT + tile, 0:LANES] = v[:, ls]

    low_lanes = lax.broadcasted_iota(jnp.int32, (QBLK, LANES), 1) < HEAD_DIM
    n_steps = (tile // QBLK) * n_heads

    def step_index(i):
        pair = (i // 2) % n_pairs
        r0 = (i // n_heads) * QBLK
        if not isinstance(i, int):
            r0 = pl.multiple_of(r0, QBLK)
        return pair, r0

    def band_lanes(row):
        first = (row // CHUNK) * CHUNK
        return pl.ds(first // LANES * LANES, -(-(LEFT + CHUNK + first % LANES) // LANES) * LANES)

    def scores(i, slot):
        pair, r0 = step_index(i)
        s = _dot_nt(q_s[pair, pl.ds(r0, QBLK), :], kbuf[slot % 2, pair, pl.ds(r0, KBLK), :])
        for c0_row in range(0, QBLK, CHUNK):
            rows, cols = pl.ds(c0_row, CHUNK), band_lanes(c0_row)
            s_buf[slot, rows, cols] = s[c0_row:c0_row + CHUNK, cols.start:cols.start + cols.size] \
                + bias_ref[i % n_heads, rows, cols]

    def softmax(slot):
        for j in range(QBLK // SLAB):
            rows, cols = pl.ds(j * SLAB, SLAB), band_lanes(j * SLAB)
            s = s_buf[slot, rows, cols]
            p_buf[slot, rows, cols] = jnp.exp(s - jnp.max(s, axis=-1, keepdims=True)).astype(_BF16)

    def weighted_values(i, slot):
        pair, r0 = step_index(i)
        o = _dot(p_buf[slot], vbuf[pair, pl.ds(r0, KBLK), :])
        o = o[:, 0:LANES] * (1.0 / o[:, LANES:LANES + 1])
        rows = pl.ds(r0, QBLK)
        if slot % 2 == 0:
            yc_s[pair, rows, :] = o
        else:
            yc_s[pair, rows, :] = jnp.where(low_lanes, yc_s[pair, rows, :], o)

    def pipeline_body(i0, slot0, do_pv, do_sm, do_qk):
        if do_sm:
            softmax(slot0)
            softmax(slot0 + 1)
        if do_pv:
            weighted_values(i0 - 2, (slot0 + 2) % N_SLOTS)
            weighted_values(i0 - 1, (slot0 + 3) % N_SLOTS)
        if do_qk:
            scores(i0 + 2, (slot0 + 2) % N_SLOTS)
            scores(i0 + 3, (slot0 + 3) % N_SLOTS)

    assert n_steps % N_SLOTS == 0 and n_steps >= 2 * N_SLOTS and N_SLOTS == 4
    pipeline_body(-2, 2, False, False, True)
    ya, yb = conv_and_pool()
    pipeline_body(0, 0, False, True, True)

    def steady(j, carry):
        pipeline_body(N_SLOTS * j + 2, 2, True, True, True)
        pipeline_body(N_SLOTS * j + 4, 0, True, True, True)
        return carry

    lax.fori_loop(0, n_steps // N_SLOTS - 1, steady, 0)
    pipeline_body(n_steps - 2, 2, True, True, False)
    pipeline_body(n_steps, 0, True, False, False)

    yc = jnp.concatenate([yc_s[pr] for pr in range(n_pairs)], axis=-1)

    gg = gg_ref[...]
    y = jnp.concatenate([
        _rms(ya, gg[:, 0:d_conv]),
        _rms(yb, gg[:, d_conv:d_conv + d_pool]),
        _rms(yc, gg[:, d_conv + d_pool:]),
    ], axis=-1).astype(_BF16)
    mix = _dot(y, wout_ref[...])
    o_ref[...] = x + _rms(mix, g2_ref[...])


def _ffn_kernel(h_ref, g1_ref, g2_ref, wgu_ref, wd_ref, o_ref, *, d_ff, chunk):
    for r in range(0, h_ref.shape[0], FFN_ROWS):
        x = h_ref[r:r + FFN_ROWS, :]
        hn = _rms(x, g1_ref[...]).astype(_BF16)
        acc = None
        for f0 in range(0, d_ff, chunk):
            f1 = min(f0 + chunk, d_ff)
            g = _dot(hn, wgu_ref[:, f0:f1])
            u = _dot(hn, wgu_ref[:, d_ff + f0:d_ff + f1])
            ff = (g * (1.0 / (1.0 + jnp.exp(-g))) * u).astype(_BF16)
            part = _dot(ff, wd_ref[f0:f1, :])
            acc = part if acc is None else acc + part
        o_ref[r:r + FFN_ROWS, :] = x + _rms(acc, g2_ref[...])


def _bias_table(rel_bias):
    period = 1 << (QBLK + KBLK - 1).bit_length()
    assert period >= QBLK + KBLK
    lead = rel_bias.shape[:-1]
    far = jnp.broadcast_to(rel_bias[..., 2 * REL_CLIP:], lead + (LEFT - REL_CLIP + 1,))
    ramp = rel_bias[..., 2 * REL_CLIP - 1::-1]
    near = jnp.broadcast_to(rel_bias[..., :1], lead + (KBLK - LEFT - REL_CLIP,))
    wrap = jnp.broadcast_to(rel_bias[..., 2 * REL_CLIP:], lead + (period - KBLK - 1,))
    vec = jnp.concatenate([far, ramp, near, wrap], axis=-1)
    assert vec.shape[-1] == period
    flat = jnp.tile(vec, (1,) * len(lead) + (QBLK,))[..., :QBLK * (period - 1)]
    toeplitz = flat.reshape(lead + (QBLK, period - 1))[..., :KBLK]
    kj = np.arange(KBLK)[None, :] - (np.arange(QBLK)[:, None] // CHUNK) * CHUNK
    in_band = (kj >= 0) & (kj < LEFT + CHUNK)
    return jnp.where(in_band, toeplitz, NEG).astype(_F32)


def _pool_block_diag(pool_w):
    n_layers, n_groups, c, _ = pool_w.shape
    eye = jnp.eye(n_groups, dtype=pool_w.dtype)
    bd = pool_w[:, :, :, None, :] * eye[None, :, None, :, None]
    return bd.reshape(n_layers, n_groups * c, n_groups * c)


def _resident(shape, layer):
    zeros = (0,) * (len(shape) - 1)
    return pl.BlockSpec((None,) + tuple(shape[1:]), lambda *_: (layer,) + zeros,
                        pipeline_mode=pl.Buffered(1))


def kernel(x, w_in, w_out, conv_w, pool_w, pool_scale, rel_bias, group_gain, pre_mix_g,
           post_mix_g, pre_ffn_g, post_ffn_g, w_gate_up, w_down):
    b, s, d_model = x.shape
    depth = w_in.shape[0]
    d_conv = conv_w.shape[2]
    d_pool = pool_scale.shape[1]
    n_heads = rel_bias.shape[1]
    d_attn = n_heads * HEAD_DIM
    n_pairs = n_heads // 2
    d_ff = w_down.shape[1]
    assert s % MIX_TILE == 0 and MIX_TILE % QBLK == 0 and MIX_TILE >= LEFT
    assert (b * s) % FFN_TILE == 0 and FFN_TILE % FFN_ROWS == 0
    assert n_heads % 2 == 0 and 2 * HEAD_DIM == LANES
    assert w_in.shape[2] == 3 * d_conv + d_pool + 3 * d_attn

    w_in_b = w_in.astype(_BF16)
    w_out_b = w_out.astype(_BF16)
    w_gu_b = w_gate_up.astype(_BF16)
    w_down_b = w_down.astype(_BF16)
    pool_bd = _pool_block_diag(pool_w).astype(_BF16)
    bias_tbl = _bias_table(rel_bias)
    row = lambda a: a.reshape(depth, 1, a.shape[-1])
    pool_scale, group_gain, pre_mix_g, post_mix_g, pre_ffn_g, post_ffn_g = map(
        row, (pool_scale, group_gain, pre_mix_g, post_mix_g, pre_ffn_g, post_ffn_g))

    params = pltpu.CompilerParams(dimension_semantics=("arbitrary", "arbitrary"),
                                  vmem_limit_bytes=VMEM_LIMIT_BYTES)
    ffn_params = pltpu.CompilerParams(dimension_semantics=("arbitrary",),
                                      vmem_limit_bytes=VMEM_LIMIT_BYTES)
    x_spec = pl.BlockSpec((None, MIX_TILE, d_model), lambda bi, ti: (bi, ti, 0))
    tok_spec = pl.BlockSpec((FFN_TILE, d_model), lambda i: (i, 0))

    h = x
    for l in range(depth):
        mix_in = (w_in_b, w_out_b, conv_w, pool_bd, pool_scale, bias_tbl, group_gain, pre_mix_g, post_mix_g)
        h = pl.pallas_call(
            functools.partial(_mix_kernel, tile=MIX_TILE, n_heads=n_heads),
            out_shape=jax.ShapeDtypeStruct((b, s, d_model), _F32),
            grid=(b, s // MIX_TILE),
            in_specs=[x_spec] + [_resident(a.shape, l) for a in mix_in],
            out_specs=x_spec,
            scratch_shapes=[
                pltpu.VMEM((n_pairs, MIX_TILE, 2 * LANES), _BF16),
                pltpu.VMEM((2, n_pairs, LEFT + MIX_TILE, 2 * LANES), _BF16),
                pltpu.VMEM((n_pairs, LEFT + MIX_TILE, 2 * LANES), _BF16),
                pltpu.VMEM((CONV_HALO + MIX_TILE, d_conv), _F32),
                pltpu.VMEM((POOL_HALO + MIX_TILE, d_pool), _F32),
                pltpu.VMEM((N_SLOTS, QBLK, KBLK), _F32),
                pltpu.VMEM((N_SLOTS, QBLK, KBLK), _BF16),
                pltpu.VMEM((n_pairs, MIX_TILE, LANES), _F32),
            ],
            compiler_params=params,
            name=f"mix{l}",
        )(h, *mix_in)

        ffn_in = (pre_ffn_g, post_ffn_g, w_gu_b, w_down_b)
        h = pl.pallas_call(
            functools.partial(_ffn_kernel, d_ff=d_ff, chunk=FFN_CHUNK),
            out_shape=jax.ShapeDtypeStruct((b * s, d_model), _F32),
            grid=((b * s) // FFN_TILE,),
            in_specs=[tok_spec] + [_resident(a.shape, l) for a in ffn_in],
            out_specs=tok_spec,
            compiler_params=ffn_params,
            name=f"ffn{l}",
        )(h.reshape(b * s, d_model), *ffn_in).reshape(b, s, d_model)
    return h
```

```python
import functools

import numpy as np
import jax
import jax.numpy as jnp
from jax import lax
from jax.experimental import pallas as pl
from jax.experimental.pallas import tpu as pltpu

CHUNK = 64
HEAD_DIM = 64
CONV_W = 3
POOL_WINDOWS = (2, 4, 8, 16)
LEFT_CHUNKS = 8
REL_CLIP = 128
EPS = 1e-6

LANES = 128
LEFT = LEFT_CHUNKS * CHUNK
QBLK = 4 * CHUNK
KBLK = QBLK + LEFT
SLAB = 32
N_SLOTS = 4
POOL_HALO = 16
CONV_HALO = 8
NEG = -1e30

MIX_TILE = 1024
FFN_TILE = 1024
FFN_ROWS = 256
FFN_CHUNK = 1024
VMEM_LIMIT_BYTES = 60 * 1024 * 1024

_F32 = jnp.float32
_BF16 = jnp.bfloat16


def _rms(x, g):
    return x * lax.rsqrt(jnp.mean(x * x, axis=-1, keepdims=True) + EPS) * g


def _dot(a, b):
    return jnp.dot(a, b, preferred_element_type=_F32)


def _dot_nt(a, b):
    return lax.dot_general(a, b, (((1,), (1,)), ((), ())), preferred_element_type=_F32)


def _mix_kernel(x_ref, win_ref, wout_ref, convw_ref, poolw_ref, pscale_ref, bias_ref,
                gg_ref, g1_ref, g2_ref, o_ref, q_s, kbuf, vbuf, zbuf, pbuf, s_buf, p_buf,
                yc_s, *, tile, n_heads):
    t = pl.program_id(1)
    d_conv = zbuf.shape[1]
    d_pool = pbuf.shape[1]
    n_pairs = n_heads // 2
    d_attn = n_heads * HEAD_DIM
    c0 = 3 * d_conv + d_pool

    @pl.when(t > 0)
    def _():
        kbuf[:, 0:LEFT, 0:LANES] = kbuf[:, tile:tile + LEFT, 0:LANES]
        vbuf[:, 0:LEFT, 0:LANES] = vbuf[:, tile:tile + LEFT, 0:LANES]

    @pl.when(t == 0)
    def _():
        lane0 = lax.broadcasted_iota(jnp.int32, (LEFT + tile, LANES), 1) == 0
        missing = lax.broadcasted_iota(jnp.int32, (LEFT + tile, LANES), 0) < LEFT
        kneg = jnp.where(lane0 & missing, NEG, 0.0).astype(_BF16)
        one = jnp.where(lane0, 1.0, 0.0).astype(_BF16)
        qone = jnp.where(lax.broadcasted_iota(jnp.int32, (tile, LANES), 1) == 0, 1.0, 0.0).astype(_BF16)
        for pr in range(n_pairs):
            q_s[0, pr, :, LANES:] = qone
            q_s[1, pr, :, LANES:] = qone
            kbuf[pr, 0:LEFT, 0:LANES] = jnp.zeros((LEFT, LANES), _BF16)
            kbuf[pr, :, LANES:] = kneg
            vbuf[pr, 0:LEFT, 0:LANES] = jnp.zeros((LEFT, LANES), _BF16)
            vbuf[pr, :, LANES:] = one
        zbuf[0:CONV_HALO, :] = jnp.zeros((CONV_HALO, d_conv), _F32)
        pbuf[0:POOL_HALO, :] = jnp.zeros((POOL_HALO, d_pool), _F32)
        p_buf[...] = jnp.zeros(p_buf.shape, _BF16)

    @pl.when(t == 1)
    def _():
        for pr in range(n_pairs):
            kbuf[pr, 0:LEFT, LANES:] = jnp.zeros((LEFT, LANES), _BF16)

    x = x_ref[...]
    xn = _rms(x, g1_ref[...]).astype(_BF16)

    pa = _dot(xn, win_ref[:, 0:c0])

    def conv_and_pool():
        gb = pa[:, 0:d_conv]
        z = pa[:, d_conv:2 * d_conv] * pa[:, 2 * d_conv:3 * d_conv]
        pu = pa[:, 3 * d_conv:c0]
        zbuf[CONV_HALO:CONV_HALO + tile, :] = z
        ze = zbuf[...]
        conv = convw_ref[2:3, :] * z
        conv = conv + convw_ref[1:2, :] * pltpu.roll(ze, 1, axis=0)[CONV_HALO:, :]
        conv = conv + convw_ref[0:1, :] * pltpu.roll(ze, 2, axis=0)[CONV_HALO:, :]
        ya = gb * conv
        zbuf[0:CONV_HALO, :] = ze[tile:tile + CONV_HALO, :]

        pbuf[POOL_HALO:POOL_HALO + tile, :] = pu
        pe = pbuf[...]
        assert POOL_WINDOWS == (2, 4, 8, 16) and d_pool == 2 * LANES
        w2 = pe + pltpu.roll(pe, 1, axis=0)
        w4 = w2 + pltpu.roll(w2, 2, axis=0)
        w4hi = w4[:, LANES:]
        w8 = w4hi + pltpu.roll(w4hi, 4, axis=0)
        w16 = w8 + pltpu.roll(w8, 8, axis=0)
        low_half = lax.broadcasted_iota(jnp.int32, (tile, LANES), 1) < LANES // 2
        wsum = jnp.concatenate([
            jnp.where(low_half, w2[POOL_HALO:, :LANES], w4[POOL_HALO:, :LANES]),
            jnp.where(low_half, w8[POOL_HALO:, :], w16[POOL_HALO:, :]),
        ], axis=-1)
        win = jnp.concatenate([jnp.where(low_half, 2, 4), jnp.where(low_half, 8, 16)], axis=-1)
        pos = t * tile + lax.broadcasted_iota(jnp.int32, (tile, d_pool), 0)
        cnt = jnp.minimum(pos + 1, win).astype(_F32)
        dpool = wsum / cnt - pu
        yb = _dot(dpool.astype(_BF16), poolw_ref[...]) * pscale_ref[...]
        pbuf[0:POOL_HALO, :] = pe[tile:tile + POOL_HALO, :]
        return ya, yb

    qkv = _dot(xn, win_ref[:, c0:c0 + 3 * d_attn])
    q = qkv[:, 0:d_attn] * (HEAD_DIM ** -0.5)
    k = qkv[:, d_attn:2 * d_attn].astype(_BF16)
    v = qkv[:, 2 * d_attn:3 * d_attn].astype(_BF16)
    even_head = (lax.broadcasted_iota(jnp.int32, (tile, d_attn), 1) // HEAD_DIM) % 2 == 0
    q_even = jnp.where(even_head, q, 0.0).astype(_BF16)
    q_odd = jnp.where(even_head, 0.0, q).astype(_BF16)
    for pr in range(n_pairs):
        ls = slice(pr * LANES, (pr + 1) * LANES)
        q_s[0, pr, :, 0:LANES] = q_even[:, ls]
        q_s[1, pr, :, 0:LANES] = q_odd[:, ls]
        kbuf[pr, LEFT:LEFT + tile, 0:LANES] = k[:, ls]
        vbuf[pr, LEFT:LEFT + tile, 0:LANES] = v[:, ls]

    low_lanes = lax.broadcasted_iota(jnp.int32, (QBLK, LANES), 1) < HEAD_DIM
    n_steps = (tile // QBLK) * n_heads

    def step_index(i):
        pair = (i // 2) % n_pairs
        r0 = (i // n_heads) * QBLK
        if not isinstance(i, int):
            r0 = pl.multiple_of(r0, QBLK)
        return pair, r0

    def band_lanes(row):
        first = (row // CHUNK) * CHUNK
        return pl.ds(first // LANES * LANES, -(-(LEFT + CHUNK + first % LANES) // LANES) * LANES)

    def scores(i, slot):
        pair, r0 = step_index(i)
        qq = jnp.concatenate([q_s[0, pair, pl.ds(r0, QBLK), :], q_s[1, pair, pl.ds(r0, QBLK), :]], axis=0)
        s = _dot_nt(qq, kbuf[pair, pl.ds(r0, KBLK), :])
        for e in range(2):
            for c0_row in range(0, QBLK, CHUNK):
                rows, cols = pl.ds(c0_row, CHUNK), band_lanes(c0_row)
                s_buf[slot + e, rows, cols] = \
                    s[e * QBLK + c0_row:e * QBLK + c0_row + CHUNK, cols.start:cols.start + cols.size] \
                    + bias_ref[(i + e) % n_heads, rows, cols]

    def softmax(slot):
        for j in range(QBLK // SLAB):
            rows, cols = pl.ds(j * SLAB, SLAB), band_lanes(j * SLAB)
            s = s_buf[slot, rows, cols]
            p_buf[slot, rows, cols] = jnp.exp(s - jnp.max(s, axis=-1, keepdims=True)).astype(_BF16)

    def weighted_values(i, slot):
        pair, r0 = step_index(i)
        o = _dot(p_buf[slot], vbuf[pair, pl.ds(r0, KBLK), :])
        o = o[:, 0:LANES] * (1.0 / o[:, LANES:LANES + 1])
        rows = pl.ds(r0, QBLK)
        if slot % 2 == 0:
            yc_s[pair, rows, :] = o
        else:
            yc_s[pair, rows, :] = jnp.where(low_lanes, yc_s[pair, rows, :], o)

    def pipeline_body(i0, slot0, do_pv, do_sm, do_qk):
        if do_sm:
            softmax(slot0)
            softmax(slot0 + 1)
        if do_pv:
            weighted_values(i0 - 2, (slot0 + 2) % N_SLOTS)
            weighted_values(i0 - 1, (slot0 + 3) % N_SLOTS)
        if do_qk:
            scores(i0 + 2, (slot0 + 2) % N_SLOTS)

    assert n_steps % N_SLOTS == 0 and n_steps >= 2 * N_SLOTS and N_SLOTS == 4
    pipeline_body(-2, 2, False, False, True)
    ya, yb = conv_and_pool()
    pipeline_body(0, 0, False, True, True)

    def steady(j, carry):
        pipeline_body(N_SLOTS * j + 2, 2, True, True, True)
        pipeline_body(N_SLOTS * j + 4, 0, True, True, True)
        return carry

    lax.fori_loop(0, n_steps // N_SLOTS - 1, steady, 0)
    pipeline_body(n_steps - 2, 2, True, True, False)
    pipeline_body(n_steps, 0, True, False, False)

    yc = jnp.concatenate([yc_s[pr] for pr in range(n_pairs)], axis=-1)

    gg = gg_ref[...]
    y = jnp.concatenate([
        _rms(ya, gg[:, 0:d_conv]),
        _rms(yb, gg[:, d_conv:d_conv + d_pool]),
        _rms(yc, gg[:, d_conv + d_pool:]),
    ], axis=-1).astype(_BF16)
    mix = _dot(y, wout_ref[...])
    o_ref[...] = x + _rms(mix, g2_ref[...])


def _ffn_kernel(h_ref, g1_ref, g2_ref, wgu_ref, wd_ref, o_ref, *, d_ff, chunk):
    for r in range(0, h_ref.shape[0], FFN_ROWS):
        x = h_ref[r:r + FFN_ROWS, :]
        hn = _rms(x, g1_ref[...]).astype(_BF16)
        acc = None
        for f0 in range(0, d_ff, chunk):
            f1 = min(f0 + chunk, d_ff)
            g = _dot(hn, wgu_ref[:, f0:f1])
            u = _dot(hn, wgu_ref[:, d_ff + f0:d_ff + f1])
            ff = (g * (1.0 / (1.0 + jnp.exp(-g))) * u).astype(_BF16)
            part = _dot(ff, wd_ref[f0:f1, :])
            acc = part if acc is None else acc + part
        o_ref[r:r + FFN_ROWS, :] = x + _rms(acc, g2_ref[...])


def _bias_table(rel_bias):
    period = 1 << (QBLK + KBLK - 1).bit_length()
    assert period >= QBLK + KBLK
    lead = rel_bias.shape[:-1]
    far = jnp.broadcast_to(rel_bias[..., 2 * REL_CLIP:], lead + (LEFT - REL_CLIP + 1,))
    ramp = rel_bias[..., 2 * REL_CLIP - 1::-1]
    near = jnp.broadcast_to(rel_bias[..., :1], lead + (KBLK - LEFT - REL_CLIP,))
    wrap = jnp.broadcast_to(rel_bias[..., 2 * REL_CLIP:], lead + (period - KBLK - 1,))
    vec = jnp.concatenate([far, ramp, near, wrap], axis=-1)
    assert vec.shape[-1] == period
    flat = jnp.tile(vec, (1,) * len(lead) + (CHUNK,))[..., :CHUNK * (period - 1)]
    block = flat.reshape(lead + (CHUNK, period - 1))
    block = jnp.concatenate([block, jnp.zeros(lead + (CHUNK, 1), block.dtype)], axis=-1)
    toeplitz = jnp.concatenate(
        [jnp.concatenate([block[..., period - c:], block[..., :KBLK - c]], axis=-1) if c else block[..., :KBLK]
         for c in range(0, QBLK, CHUNK)], axis=-2)
    kj = np.arange(KBLK)[None, :] - (np.arange(QBLK)[:, None] // CHUNK) * CHUNK
    in_band = (kj >= 0) & (kj < LEFT + CHUNK)
    return jnp.where(in_band, toeplitz, NEG).astype(_F32)


def _pool_block_diag(pool_w):
    n_layers, n_groups, c, _ = pool_w.shape
    eye = jnp.eye(n_groups, dtype=pool_w.dtype)
    bd = pool_w[:, :, :, None, :] * eye[None, :, None, :, None]
    return bd.reshape(n_layers, n_groups * c, n_groups * c)


def _resident(shape, layer):
    zeros = (0,) * (len(shape) - 1)
    return pl.BlockSpec((None,) + tuple(shape[1:]), lambda *_: (layer,) + zeros,
                        pipeline_mode=pl.Buffered(1))


def kernel(x, w_in, w_out, conv_w, pool_w, pool_scale, rel_bias, group_gain, pre_mix_g,
           post_mix_g, pre_ffn_g, post_ffn_g, w_gate_up, w_down):
    b, s, d_model = x.shape
    depth = w_in.shape[0]
    d_conv = conv_w.shape[2]
    d_pool = pool_scale.shape[1]
    n_heads = rel_bias.shape[1]
    d_attn = n_heads * HEAD_DIM
    n_pairs = n_heads // 2
    d_ff = w_down.shape[1]
    assert s % MIX_TILE == 0 and MIX_TILE % QBLK == 0 and MIX_TILE >= LEFT
    assert (b * s) % FFN_TILE == 0 and FFN_TILE % FFN_ROWS == 0
    assert n_heads % 2 == 0 and 2 * HEAD_DIM == LANES
    assert w_in.shape[2] == 3 * d_conv + d_pool + 3 * d_attn

    w_in_b = w_in.astype(_BF16)
    w_out_b = w_out.astype(_BF16)
    w_gu_b = w_gate_up.astype(_BF16)
    w_down_b = w_down.astype(_BF16)
    pool_bd = _pool_block_diag(pool_w).astype(_BF16)
    bias_tbl = _bias_table(rel_bias)
    row = lambda a: a.reshape(depth, 1, a.shape[-1])
    pool_scale, group_gain, pre_mix_g, post_mix_g, pre_ffn_g, post_ffn_g = map(
        row, (pool_scale, group_gain, pre_mix_g, post_mix_g, pre_ffn_g, post_ffn_g))

    params = pltpu.CompilerParams(dimension_semantics=("arbitrary", "arbitrary"),
                                  vmem_limit_bytes=VMEM_LIMIT_BYTES)
    ffn_params = pltpu.CompilerParams(dimension_semantics=("arbitrary",),
                                      vmem_limit_bytes=VMEM_LIMIT_BYTES)
    x_spec = pl.BlockSpec((None, MIX_TILE, d_model), lambda bi, ti: (bi, ti, 0))
    tok_spec = pl.BlockSpec((FFN_TILE, d_model), lambda i: (i, 0))

    h = x
    for l in range(depth):
        mix_in = (w_in_b, w_out_b, conv_w, pool_bd, pool_scale, bias_tbl, group_gain, pre_mix_g, post_mix_g)
        h = pl.pallas_call(
            functools.partial(_mix_kernel, tile=MIX_TILE, n_heads=n_heads),
            out_shape=jax.ShapeDtypeStruct((b, s, d_model), _F32),
            grid=(b, s // MIX_TILE),
            in_specs=[x_spec] + [_resident(a.shape, l) for a in mix_in],
            out_specs=x_spec,
            scratch_shapes=[
                pltpu.VMEM((2, n_pairs, MIX_TILE, 2 * LANES), _BF16),
                pltpu.VMEM((n_pairs, LEFT + MIX_TILE, 2 * LANES), _BF16),
                pltpu.VMEM((n_pairs, LEFT + MIX_TILE, 2 * LANES), _BF16),
                pltpu.VMEM((CONV_HALO + MIX_TILE, d_conv), _F32),
                pltpu.VMEM((POOL_HALO + MIX_TILE, d_pool), _F32),
                pltpu.VMEM((N_SLOTS, QBLK, KBLK), _F32),
                pltpu.VMEM((N_SLOTS, QBLK, KBLK), _BF16),
                pltpu.VMEM((n_pairs, MIX_TILE, LANES), _F32),
            ],
            compiler_params=params,
            name=f"mix{l}",
        )(h, *mix_in)

        ffn_in = (pre_ffn_g, post_ffn_g, w_gu_b, w_down_b)
        h = pl.pallas_call(
            functools.partial(_ffn_kernel, d_ff=d_ff, chunk=FFN_CHUNK),
            out_shape=jax.ShapeDtypeStruct((b * s, d_model), _F32),
            grid=((b * s) // FFN_TILE,),
            in_specs=[tok_spec] + [_resident(a.shape, l) for a in ffn_in],
            out_specs=tok_spec,
            compiler_params=ffn_params,
            name=f"ffn{l}",
        )(h.reshape(b * s, d_model), *ffn_in).reshape(b, s, d_model)
    return h
```

```python
import functools

import numpy as np
import jax
import jax.numpy as jnp
from jax import lax
from jax.experimental import pallas as pl
from jax.experimental.pallas import tpu as pltpu

CHUNK = 64
HEAD_DIM = 64
CONV_W = 3
POOL_WINDOWS = (2, 4, 8, 16)
LEFT_CHUNKS = 8
REL_CLIP = 128
EPS = 1e-6

LANES = 128
LEFT = LEFT_CHUNKS * CHUNK
QBLK = 4 * CHUNK
KBLK = QBLK + LEFT
SLAB = 32
N_SLOTS = 4
POOL_HALO = 16
CONV_HALO = 8
NEG = -1e30

MIX_TILE = 1024
FFN_TILE = 1024
FFN_ROWS = 256
FFN_CHUNK = 1024
VMEM_LIMIT_BYTES = 60 * 1024 * 1024

_F32 = jnp.float32
_BF16 = jnp.bfloat16


def _rms(x, g):
    return x * lax.rsqrt(jnp.mean(x * x, axis=-1, keepdims=True) + EPS) * g


def _dot(a, b):
    return jnp.dot(a, b, preferred_element_type=_F32)


def _dot_nt(a, b):
    return lax.dot_general(a, b, (((1,), (1,)), ((), ())), preferred_element_type=_F32)


def _mix_kernel(x_ref, win_ref, wout_ref, convw_ref, poolw_ref, pscale_ref, bias_ref,
                gg_ref, g1_ref, g2_ref, o_ref, q_s, kbuf, vbuf, zbuf, pbuf, s_buf, p_buf,
                yc_s, *, tile, n_heads):
    t = pl.program_id(1)
    d_conv = zbuf.shape[1]
    d_pool = pbuf.shape[1]
    n_pairs = n_heads // 2
    d_attn = n_heads * HEAD_DIM
    c0 = 3 * d_conv + d_pool

    @pl.when(t > 0)
    def _():
        kbuf[:, 0:LEFT, 0:LANES] = kbuf[:, tile:tile + LEFT, 0:LANES]
        vbuf[:, 0:LEFT, 0:LANES] = vbuf[:, tile:tile + LEFT, 0:LANES]

    @pl.when(t == 0)
    def _():
        lane0 = lax.broadcasted_iota(jnp.int32, (LEFT + tile, LANES), 1) == 0
        missing = lax.broadcasted_iota(jnp.int32, (LEFT + tile, LANES), 0) < LEFT
        kneg = jnp.where(lane0 & missing, NEG, 0.0).astype(_BF16)
        one = jnp.where(lane0, 1.0, 0.0).astype(_BF16)
        qone = jnp.where(lax.broadcasted_iota(jnp.int32, (tile, LANES), 1) == 0, 1.0, 0.0).astype(_BF16)
        for pr in range(n_pairs):
            q_s[0, pr, :, LANES:] = qone
            q_s[1, pr, :, LANES:] = qone
            kbuf[pr, 0:LEFT, 0:LANES] = jnp.zeros((LEFT, LANES), _BF16)
            kbuf[pr, :, LANES:] = kneg
            vbuf[pr, 0:LEFT, 0:LANES] = jnp.zeros((LEFT, LANES), _BF16)
            vbuf[pr, :, LANES:] = one
        zbuf[0:CONV_HALO, :] = jnp.zeros((CONV_HALO, d_conv), _F32)
        pbuf[0:POOL_HALO, :] = jnp.zeros((POOL_HALO, d_pool), _F32)
        p_buf[...] = jnp.zeros(p_buf.shape, _BF16)

    @pl.when(t == 1)
    def _():
        for pr in range(n_pairs):
            kbuf[pr, 0:LEFT, LANES:] = jnp.zeros((LEFT, LANES), _BF16)

    x = x_ref[...]
    xn = _rms(x, g1_ref[...]).astype(_BF16)

    pa = _dot(xn, win_ref[:, 0:c0])

    def conv_and_pool():
        gb = pa[:, 0:d_conv]
        z = pa[:, d_conv:2 * d_conv] * pa[:, 2 * d_conv:3 * d_conv]
        pu = pa[:, 3 * d_conv:c0]
        zbuf[CONV_HALO:CONV_HALO + tile, :] = z
        ze = zbuf[...]
        conv = convw_ref[2:3, :] * z
        conv = conv + convw_ref[1:2, :] * pltpu.roll(ze, 1, axis=0)[CONV_HALO:, :]
        conv = conv + convw_ref[0:1, :] * pltpu.roll(ze, 2, axis=0)[CONV_HALO:, :]
        ya = gb * conv
        zbuf[0:CONV_HALO, :] = ze[tile:tile + CONV_HALO, :]

        pbuf[POOL_HALO:POOL_HALO + tile, :] = pu
        pe = pbuf[...]
        assert POOL_WINDOWS == (2, 4, 8, 16) and d_pool == 2 * LANES
        w2 = pe + pltpu.roll(pe, 1, axis=0)
        w4 = w2 + pltpu.roll(w2, 2, axis=0)
        w4hi = w4[:, LANES:]
        w8 = w4hi + pltpu.roll(w4hi, 4, axis=0)
        w16 = w8 + pltpu.roll(w8, 8, axis=0)
        low_half = lax.broadcasted_iota(jnp.int32, (tile, LANES), 1) < LANES // 2
        wsum = jnp.concatenate([
            jnp.where(low_half, w2[POOL_HALO:, :LANES], w4[POOL_HALO:, :LANES]),
            jnp.where(low_half, w8[POOL_HALO:, :], w16[POOL_HALO:, :]),
        ], axis=-1)
        win = jnp.concatenate([jnp.where(low_half, 2, 4), jnp.where(low_half, 8, 16)], axis=-1)
        pos = t * tile + lax.broadcasted_iota(jnp.int32, (tile, d_pool), 0)
        cnt = jnp.minimum(pos + 1, win).astype(_F32)
        dpool = wsum / cnt - pu
        yb = _dot(dpool.astype(_BF16), poolw_ref[...]) * pscale_ref[...]
        pbuf[0:POOL_HALO, :] = pe[tile:tile + POOL_HALO, :]
        return ya, yb

    qkv = _dot(xn, win_ref[:, c0:c0 + 3 * d_attn])
    q = qkv[:, 0:d_attn] * (HEAD_DIM ** -0.5)
    k = qkv[:, d_attn:2 * d_attn].astype(_BF16)
    v = qkv[:, 2 * d_attn:3 * d_attn].astype(_BF16)
    even_head = (lax.broadcasted_iota(jnp.int32, (tile, d_attn), 1) // HEAD_DIM) % 2 == 0
    q_even = jnp.where(even_head, q, 0.0).astype(_BF16)
    q_odd = jnp.where(even_head, 0.0, q).astype(_BF16)
    for pr in range(n_pairs):
        ls = slice(pr * LANES, (pr + 1) * LANES)
        q_s[0, pr, :, 0:LANES] = q_even[:, ls]
        q_s[1, pr, :, 0:LANES] = q_odd[:, ls]
        kbuf[pr, LEFT:LEFT + tile, 0:LANES] = k[:, ls]
        vbuf[pr, LEFT:LEFT + tile, 0:LANES] = v[:, ls]

    low_lanes = lax.broadcasted_iota(jnp.int32, (QBLK, LANES), 1) < HEAD_DIM
    n_steps = (tile // QBLK) * n_heads

    def step_index(i):
        pair = (i // 2) % n_pairs
        r0 = (i // n_heads) * QBLK
        if not isinstance(i, int):
            r0 = pl.multiple_of(r0, QBLK)
        return pair, r0

    def band_lanes(row):
        first = (row // CHUNK) * CHUNK
        return pl.ds(first // LANES * LANES, -(-(LEFT + CHUNK + first % LANES) // LANES) * LANES)

    def scores(i, slot):
        pair, r0 = step_index(i)
        qq = jnp.concatenate([q_s[0, pair, pl.ds(r0, QBLK), :], q_s[1, pair, pl.ds(r0, QBLK), :]], axis=0)
        s = _dot_nt(qq, kbuf[pair, pl.ds(r0, KBLK), :])
        for e in range(2):
            for c0_row in range(0, QBLK, CHUNK):
                rows, cols = pl.ds(c0_row, CHUNK), band_lanes(c0_row)
                s_buf[slot + e, rows, cols] = \
                    s[e * QBLK + c0_row:e * QBLK + c0_row + CHUNK, cols.start:cols.start + cols.size] \
                    + bias_ref[(i + e) % n_heads, rows, cols]

    def softmax(slot):
        for j in range(QBLK // SLAB):
            rows, cols = pl.ds(j * SLAB, SLAB), band_lanes(j * SLAB)
            s = s_buf[slot, rows, cols]
            p_buf[slot, rows, cols] = jnp.exp(s - jnp.max(s, axis=-1, keepdims=True)).astype(_BF16)

    def weighted_values(i, slot):
        pair, r0 = step_index(i)
        o = _dot(p_buf[slot], vbuf[pair, pl.ds(r0, KBLK), :])
        o = o[:, 0:LANES] * (1.0 / o[:, LANES:LANES + 1])
        rows = pl.ds(r0, QBLK)
        if slot % 2 == 0:
            yc_s[pair, rows, :] = o
        else:
            yc_s[pair, rows, :] = jnp.where(low_lanes, yc_s[pair, rows, :], o)

    def pipeline_body(i0, slot0, do_pv, do_sm, do_qk):
        if do_sm:
            softmax(slot0)
            softmax(slot0 + 1)
        if do_pv:
            weighted_values(i0 - 2, (slot0 + 2) % N_SLOTS)
            weighted_values(i0 - 1, (slot0 + 3) % N_SLOTS)
        if do_qk:
            scores(i0 + 2, (slot0 + 2) % N_SLOTS)

    assert n_steps % N_SLOTS == 0 and n_steps >= 2 * N_SLOTS and N_SLOTS == 4
    pipeline_body(-2, 2, False, False, True)
    ya, yb = conv_and_pool()
    pipeline_body(0, 0, False, True, True)

    def steady(j, carry):
        pipeline_body(N_SLOTS * j + 2, 2, True, True, True)
        pipeline_body(N_SLOTS * j + 4, 0, True, True, True)
        return carry

    lax.fori_loop(0, n_steps // N_SLOTS - 1, steady, 0)
    pipeline_body(n_steps - 2, 2, True, True, False)
    pipeline_body(n_steps, 0, True, False, False)

    yc = jnp.concatenate([yc_s[pr] for pr in range(n_pairs)], axis=-1)

    gg = gg_ref[...]
    y = jnp.concatenate([
        _rms(ya, gg[:, 0:d_conv]),
        _rms(yb, gg[:, d_conv:d_conv + d_pool]),
        _rms(yc, gg[:, d_conv + d_pool:]),
    ], axis=-1).astype(_BF16)
    mix = _dot(y, wout_ref[...])
    o_ref[...] = x_ref[...] + _rms(mix, g2_ref[...])


def _ffn_kernel(h_ref, g1_ref, g2_ref, wgu_ref, wd_ref, o_ref, *, d_ff, chunk):
    for r in range(0, h_ref.shape[0], FFN_ROWS):
        x = h_ref[r:r + FFN_ROWS, :]
        hn = _rms(x, g1_ref[...]).astype(_BF16)
        acc = None
        for f0 in range(0, d_ff, chunk):
            f1 = min(f0 + chunk, d_ff)
            g = _dot(hn, wgu_ref[:, f0:f1])
            u = _dot(hn, wgu_ref[:, d_ff + f0:d_ff + f1])
            ff = (g * (1.0 / (1.0 + jnp.exp(-g))) * u).astype(_BF16)
            part = _dot(ff, wd_ref[f0:f1, :])
            acc = part if acc is None else acc + part
        o_ref[r:r + FFN_ROWS, :] = x + _rms(acc, g2_ref[...])


def _bias_table(rel_bias):
    period = 1 << (QBLK + KBLK - 1).bit_length()
    assert period >= QBLK + KBLK
    lead = rel_bias.shape[:-1]
    far = jnp.broadcast_to(rel_bias[..., 2 * REL_CLIP:], lead + (LEFT - REL_CLIP + 1,))
    ramp = rel_bias[..., 2 * REL_CLIP - 1::-1]
    near = jnp.broadcast_to(rel_bias[..., :1], lead + (KBLK - LEFT - REL_CLIP,))
    wrap = jnp.broadcast_to(rel_bias[..., 2 * REL_CLIP:], lead + (period - KBLK - 1,))
    vec = jnp.concatenate([far, ramp, near, wrap], axis=-1)
    assert vec.shape[-1] == period
    flat = jnp.tile(vec, (1,) * len(lead) + (CHUNK,))[..., :CHUNK * (period - 1)]
    block = flat.reshape(lead + (CHUNK, period - 1))
    block = jnp.concatenate([block, jnp.zeros(lead + (CHUNK, 1), block.dtype)], axis=-1)
    toeplitz = jnp.concatenate(
        [jnp.concatenate([block[..., period - c:], block[..., :KBLK - c]], axis=-1) if c else block[..., :KBLK]
         for c in range(0, QBLK, CHUNK)], axis=-2)
    kj = np.arange(KBLK)[None, :] - (np.arange(QBLK)[:, None] // CHUNK) * CHUNK
    in_band = (kj >= 0) & (kj < LEFT + CHUNK)
    return jnp.where(in_band, toeplitz, NEG).astype(_F32)


def _pool_block_diag(pool_w):
    n_layers, n_groups, c, _ = pool_w.shape
    eye = jnp.eye(n_groups, dtype=pool_w.dtype)
    bd = pool_w[:, :, :, None, :] * eye[None, :, None, :, None]
    return bd.reshape(n_layers, n_groups * c, n_groups * c)


def _resident(shape, layer):
    zeros = (0,) * (len(shape) - 1)
    return pl.BlockSpec((None,) + tuple(shape[1:]), lambda *_: (layer,) + zeros,
                        pipeline_mode=pl.Buffered(1))


def kernel(x, w_in, w_out, conv_w, pool_w, pool_scale, rel_bias, group_gain, pre_mix_g,
           post_mix_g, pre_ffn_g, post_ffn_g, w_gate_up, w_down):
    b, s, d_model = x.shape
    depth = w_in.shape[0]
    d_conv = conv_w.shape[2]
    d_pool = pool_scale.shape[1]
    n_heads = rel_bias.shape[1]
    d_attn = n_heads * HEAD_DIM
    n_pairs = n_heads // 2
    d_ff = w_down.shape[1]
    assert s % MIX_TILE == 0 and MIX_TILE % QBLK == 0 and MIX_TILE >= LEFT
    assert (b * s) % FFN_TILE == 0 and FFN_TILE % FFN_ROWS == 0
    assert n_heads % 2 == 0 and 2 * HEAD_DIM == LANES
    assert w_in.shape[2] == 3 * d_conv + d_pool + 3 * d_attn

    w_in_b = w_in.astype(_BF16)
    w_out_b = w_out.astype(_BF16)
    w_gu_b = w_gate_up.astype(_BF16)
    w_down_b = w_down.astype(_BF16)
    pool_bd = _pool_block_diag(pool_w).astype(_BF16)
    bias_tbl = _bias_table(rel_bias)
    row = lambda a: a.reshape(depth, 1, a.shape[-1])
    pool_scale, group_gain, pre_mix_g, post_mix_g, pre_ffn_g, post_ffn_g = map(
        row, (pool_scale, group_gain, pre_mix_g, post_mix_g, pre_ffn_g, post_ffn_g))

    params = pltpu.CompilerParams(dimension_semantics=("arbitrary", "arbitrary"),
                                  vmem_limit_bytes=VMEM_LIMIT_BYTES)
    ffn_params = pltpu.CompilerParams(dimension_semantics=("arbitrary",),
                                      vmem_limit_bytes=VMEM_LIMIT_BYTES)
    x_spec = pl.BlockSpec((None, MIX_TILE, d_model), lambda bi, ti: (bi, ti, 0))
    tok_spec = pl.BlockSpec((FFN_TILE, d_model), lambda i: (i, 0))

    h = x
    for l in range(depth):
        mix_in = (w_in_b, w_out_b, conv_w, pool_bd, pool_scale, bias_tbl, group_gain, pre_mix_g, post_mix_g)
        h = pl.pallas_call(
            functools.partial(_mix_kernel, tile=MIX_TILE, n_heads=n_heads),
            out_shape=jax.ShapeDtypeStruct((b, s, d_model), _F32),
            grid=(b, s // MIX_TILE),
            in_specs=[x_spec] + [_resident(a.shape, l) for a in mix_in],
            out_specs=x_spec,
            scratch_shapes=[
                pltpu.VMEM((2, n_pairs, MIX_TILE, 2 * LANES), _BF16),
                pltpu.VMEM((n_pairs, LEFT + MIX_TILE, 2 * LANES), _BF16),
                pltpu.VMEM((n_pairs, LEFT + MIX_TILE, 2 * LANES), _BF16),
                pltpu.VMEM((CONV_HALO + MIX_TILE, d_conv), _F32),
                pltpu.VMEM((POOL_HALO + MIX_TILE, d_pool), _F32),
                pltpu.VMEM((N_SLOTS, QBLK, KBLK), _F32),
                pltpu.VMEM((N_SLOTS, QBLK, KBLK), _BF16),
                pltpu.VMEM((n_pairs, MIX_TILE, LANES), _F32),
            ],
            compiler_params=params,
            name=f"mix{l}",
        )(h, *mix_in)

        ffn_in = (pre_ffn_g, post_ffn_g, w_gu_b, w_down_b)
        h = pl.pallas_call(
            functools.partial(_ffn_kernel, d_ff=d_ff, chunk=FFN_CHUNK),
            out_shape=jax.ShapeDtypeStruct((b * s, d_model), _F32),
            grid=((b * s) // FFN_TILE,),
            in_specs=[tok_spec] + [_resident(a.shape, l) for a in ffn_in],
            out_specs=tok_spec,
            compiler_params=ffn_params,
            name=f"ffn{l}",
        )(h.reshape(b * s, d_model), *ffn_in).reshape(b, s, d_model)
    return h
```
